```python
import math
import jax, jax.numpy as jnp
from jax import lax
import numpy as np

D_MODEL = 1024
BATCH = 2
SEQ = 16384
DEPTH = 2
DEC_BATCH = 4
DEC_SEQ = 8192
PAST_LEN = 128

CHUNK = 128
D_A = 512
G_A = 4
GA_DIM = D_A // G_A
D_B = 512
H_B = 4
HB_DIM = D_B // H_B
D_C = 512
G_C = 4
GC_DIM = D_C // G_C
N_BRANCH = 3
D_FF = 2816
ROPE_BASE = 10000.0
LN_EPS = 1e-5
ALPHA = (2.0 * DEPTH) ** 0.25
BETA = (8.0 * DEPTH) ** -0.25
D_IN = N_BRANCH * D_MODEL + 2 * D_A + 4 * D_B + D_C

kernel_name = "hybrid_gmlp_retention_fnet_encoder"


def layer_norm(x, g, b):
    xf = x.astype(jnp.float32)
    mu = jnp.mean(xf, -1, keepdims=True)
    var = jnp.mean(jnp.square(xf - mu), -1, keepdims=True)
    return ((xf - mu) * lax.rsqrt(var + LN_EPS) * g + b).astype(x.dtype)


def swiglu(x, w_gate, w_up, w_down):
    return (jax.nn.silu(x @ w_gate) * (x @ w_up)) @ w_down


def rotary(x):
    L = x.shape[1]
    half = x.shape[-1] // 2
    inv = ROPE_BASE ** (-jnp.arange(half, dtype=jnp.float32) / half)
    ang = jnp.arange(L, dtype=jnp.float32)[:, None] * inv[None, :]
    cos = jnp.cos(ang)[None, :, None, :]
    sin = jnp.sin(ang)[None, :, None, :]
    x1, x2 = x[..., :half], x[..., half:]
    return jnp.concatenate([x1 * cos - x2 * sin, x1 * sin + x2 * cos], -1)


def retention_dir(q, k, v, log_gamma, include_diag):
    idx = jnp.arange(CHUNK)
    idxf = idx.astype(jnp.float32)
    rel = idx[:, None] - idx[None, :]
    mask = (rel >= 0) if include_diag else (rel > 0)
    relf = jnp.where(mask, rel, 0).astype(jnp.float32)
    dmat = jnp.where(mask[None], jnp.exp(relf[None] * log_gamma[:, None, None]), 0.0)
    scores = jnp.einsum('bnihd,bnjhd->bnhij', q, k) * dmat
    intra = jnp.einsum('bnhij,bnjhe->bnihe', scores, v)
    k_dec = k * jnp.exp((CHUNK - 1 - idxf)[:, None] * log_gamma[None, :])[:, :, None]
    kv = jnp.einsum('bnjhd,bnjhe->nbhde', k_dec, v)
    chunk_decay = jnp.exp(CHUNK * log_gamma)[None, :, None, None]

    def step(state, kv_n):
        return chunk_decay * state + kv_n, state

    _, prev = lax.scan(step, jnp.zeros_like(kv[0]), kv)
    q_dec = q * jnp.exp((idxf + 1.0)[:, None] * log_gamma[None, :])[:, :, None]
    cross = jnp.einsum('bnihd,nbhde->bnihe', q_dec, prev)
    return intra + cross


def retention_branch(q, k, v, g, logit_f, logit_b, gn_g, gn_b):
    bsz, L, _ = q.shape
    n = L // CHUNK
    heads = lambda t: t.astype(jnp.float32).reshape(bsz, L, H_B, HB_DIM)
    qh = rotary(heads(q)) * (HB_DIM ** -0.5)
    kh = rotary(heads(k))
    vh = heads(v)
    chunk = lambda t: t.reshape(bsz, n, CHUNK, H_B, HB_DIM)
    flip = lambda t: chunk(t[:, ::-1])
    lg_f = jax.nn.log_sigmoid(logit_f.astype(jnp.float32))
    lg_b = jax.nn.log_sigmoid(logit_b.astype(jnp.float32))
    fwd = retention_dir(chunk(qh), chunk(kh), chunk(vh), lg_f, True)
    bwd = retention_dir(flip(qh), flip(kh), flip(vh), lg_b, False)
    o = fwd.reshape(bsz, L, H_B, HB_DIM) + bwd.reshape(bsz, L, H_B, HB_DIM)[:, ::-1]
    mu = jnp.mean(o, -1, keepdims=True)
    var = jnp.mean(jnp.square(o - mu), -1, keepdims=True)
    o = ((o - mu) * lax.rsqrt(var + LN_EPS)).reshape(bsz, L, D_B) * gn_g + gn_b
    return (jax.nn.silu(g.astype(jnp.float32)) * o).astype(q.dtype)


def sgu_branch(z, ln_g, ln_b, w_s, b_s):
    bsz, L, _ = z.shape
    z = jax.nn.gelu(z)
    u, v = jnp.split(z, 2, axis=-1)
    v = layer_norm(v, ln_g, ln_b).reshape(bsz, L // CHUNK, CHUNK, G_A, GA_DIM)
    v = jnp.einsum('gij,bnjgc->bnigc', w_s, v) + b_s.T[:, :, None]
    return u * v.reshape(bsz, L, D_A)


def fourier_branch(z):
    bsz, L, _ = z.shape
    zf = z.astype(jnp.float32).reshape(bsz, L, G_C, GC_DIM)
    y = jnp.fft.fft2(zf, axes=(1, 3), norm="ortho").real
    return y.reshape(bsz, L, D_C).astype(z.dtype)


def token_mixer(x, w_in, b_in, sgu_ln_g, sgu_ln_b, sgu_w, sgu_b, logit_f, logit_b,
                gn_g, gn_b, w_br_a, w_br_b, w_br_c, w_o):
    h = x @ w_in + b_in
    sizes = (N_BRANCH * D_MODEL, 2 * D_A, D_B, D_B, D_B, D_B, D_C)
    cuts = [int(c) for c in np.cumsum(sizes)[:-1]]
    gates, za, q, k, v, g, zc = jnp.split(h, cuts, axis=-1)
    ya = sgu_branch(za, sgu_ln_g, sgu_ln_b, sgu_w, sgu_b) @ w_br_a
    yb = retention_branch(q, k, v, g, logit_f, logit_b, gn_g, gn_b) @ w_br_b
    yc = fourier_branch(zc) @ w_br_c
    ga, gb, gc = jnp.split(jax.nn.sigmoid(gates), N_BRANCH, axis=-1)
    return (ga * ya + gb * yb + gc * yc) @ w_o


def trunk(x, params):
    (ffn1_w_gate, ffn1_w_up, ffn1_w_down, ln1_g, ln1_b, w_in, b_in, sgu_ln_g, sgu_ln_b,
     sgu_w, sgu_b, ret_logit_fwd, ret_logit_bwd, ret_gn_g, ret_gn_b, w_br_a, w_br_b, w_br_c,
     w_o, ln2_g, ln2_b, ffn2_w_gate, ffn2_w_up, ffn2_w_down, ln3_g, ln3_b) = params
    for l in range(DEPTH):
        x = layer_norm(ALPHA * x + 0.5 * swiglu(x, ffn1_w_gate[l], ffn1_w_up[l], ffn1_w_down[l]),
                       ln1_g[l], ln1_b[l])
        mix = token_mixer(x, w_in[l], b_in[l], sgu_ln_g[l], sgu_ln_b[l], sgu_w[l], sgu_b[l],
                          ret_logit_fwd[l], ret_logit_bwd[l], ret_gn_g[l], ret_gn_b[l],
                          w_br_a[l], w_br_b[l], w_br_c[l], w_o[l])
        x = layer_norm(ALPHA * x + mix, ln2_g[l], ln2_b[l])
        x = layer_norm(ALPHA * x + 0.5 * swiglu(x, ffn2_w_gate[l], ffn2_w_up[l], ffn2_w_down[l]),
                       ln3_g[l], ln3_b[l])
    return x


def setup_inputs(seed: int = 0) -> dict:
    key = jax.random.key(seed)
    keys = list(jax.random.split(key, 40))

    def nrm(shape, scale):
        return jax.random.normal(keys.pop(), shape, jnp.float32) * scale

    def gain(shape):
        return 1.0 + nrm(shape, 0.02)

    ret_base = jnp.asarray(np.log(2.0 ** (5 + np.arange(H_B)) - 1.0), jnp.float32)[None, :]
    return {
        "x_prompt": nrm((BATCH, SEQ, D_MODEL), 1.0),
        "x_sample": nrm((DEC_BATCH, DEC_SEQ, D_MODEL), 1.0),
        "ffn1_w_gate": nrm((DEPTH, D_MODEL, D_FF), D_MODEL ** -0.5),
        "ffn1_w_up": nrm((DEPTH, D_MODEL, D_FF), D_MODEL ** -0.5),
        "ffn1_w_down": nrm((DEPTH, D_FF, D_MODEL), BETA * D_FF ** -0.5),
        "ln1_g": gain((DEPTH, D_MODEL)),
        "ln1_b": nrm((DEPTH, D_MODEL), 0.02),
        "w_in": nrm((DEPTH, D_MODEL, D_IN), D_MODEL ** -0.5),
        "b_in": nrm((DEPTH, D_IN), 0.02),
        "sgu_ln_g": gain((DEPTH, D_A)),
        "sgu_ln_b": nrm((DEPTH, D_A), 0.02),
        "sgu_w": nrm((DEPTH, G_A, CHUNK, CHUNK), CHUNK ** -0.5),
        "sgu_b": gain((DEPTH, G_A, CHUNK)),
        "ret_logit_fwd": ret_base + nrm((DEPTH, H_B), 0.1),
        "ret_logit_bwd": ret_base + nrm((DEPTH, H_B), 0.1),
        "ret_gn_g": gain((DEPTH, D_B)),
        "ret_gn_b": nrm((DEPTH, D_B), 0.02),
        "w_br_a": nrm((DEPTH, D_A, D_MODEL), D_A ** -0.5),
        "w_br_b": nrm((DEPTH, D_B, D_MODEL), D_B ** -0.5),
        "w_br_c": nrm((DEPTH, D_C, D_MODEL), D_C ** -0.5),
        "w_o": nrm((DEPTH, D_MODEL, D_MODEL), BETA * D_MODEL ** -0.5),
        "ln2_g": gain((DEPTH, D_MODEL)),
        "ln2_b": nrm((DEPTH, D_MODEL), 0.02),
        "ffn2_w_gate": nrm((DEPTH, D_MODEL, D_FF), D_MODEL ** -0.5),
        "ffn2_w_up": nrm((DEPTH, D_MODEL, D_FF), D_MODEL ** -0.5),
        "ffn2_w_down": nrm((DEPTH, D_FF, D_MODEL), BETA * D_FF ** -0.5),
        "ln3_g": gain((DEPTH, D_MODEL)),
        "ln3_b": nrm((DEPTH, D_MODEL), 0.02),
    }


def reference(x_prompt, x_sample, ffn1_w_gate, ffn1_w_up, ffn1_w_down, ln1_g, ln1_b, w_in, b_in,
              sgu_ln_g, sgu_ln_b, sgu_w, sgu_b, ret_logit_fwd, ret_logit_bwd, ret_gn_g, ret_gn_b,
              w_br_a, w_br_b, w_br_c, w_o, ln2_g, ln2_b, ffn2_w_gate, ffn2_w_up, ffn2_w_down,
              ln3_g, ln3_b):
    params = (ffn1_w_gate, ffn1_w_up, ffn1_w_down, ln1_g, ln1_b, w_in, b_in, sgu_ln_g, sgu_ln_b,
              sgu_w, sgu_b, ret_logit_fwd, ret_logit_bwd, ret_gn_g, ret_gn_b, w_br_a, w_br_b,
              w_br_c, w_o, ln2_g, ln2_b, ffn2_w_gate, ffn2_w_up, ffn2_w_down, ln3_g, ln3_b)
    y_prompt = trunk(x_prompt, params)
    y_sample = trunk(x_sample, params)
    return (y_prompt, y_sample)
```

```python
import functools
import math

import jax
import jax.numpy as jnp
from jax import lax
from jax.experimental import pallas as pl
from jax.experimental.pallas import tpu as pltpu

F32 = jnp.float32
BF16 = jnp.bfloat16

D_MODEL = 1024
DEPTH = 2
CHUNK = 128
D_A = 512
G_A = 4
GA_DIM = D_A // G_A
D_B = 512
H_B = 4
HB_DIM = D_B // H_B
D_C = 512
G_C = 4
GC_DIM = D_C // G_C
N_BRANCH = 3
D_FF = 2816
ROPE_BASE = 10000.0
LN_EPS = 1e-5
ALPHA = (2.0 * DEPTH) ** 0.25
D_IN = N_BRANCH * D_MODEL + 2 * D_A + 4 * D_B + D_C

OFF_GATE = 0
OFF_ZA = N_BRANCH * D_MODEL
OFF_Q = OFF_ZA + 2 * D_A
OFF_K = OFF_Q + D_B
OFF_V = OFF_K + D_B
OFF_G = OFF_V + D_B
OFF_ZC = OFF_G + D_B

LANES = 128
FF_CHUNK = 256
TM_FFN = 512
TM_INPROJ = 256
RET_CHUNK = 128
FFT_L2 = 128
FFT_COLS = 4096
FFT_K1B = 8
VMEM_LIMIT = 56 * 1024 * 1024


def _params(n_axes):
    return pltpu.CompilerParams(dimension_semantics=("arbitrary",) * n_axes,
                                vmem_limit_bytes=VMEM_LIMIT)


def _resident(shape):
    zeros = (0,) * len(shape)
    return pl.BlockSpec(shape, lambda *_: zeros, pipeline_mode=pl.Buffered(1))


def _layer_norm(y, g, b):
    mu = jnp.mean(y, axis=-1, keepdims=True)
    d = y - mu
    var = jnp.mean(d * d, axis=-1, keepdims=True)
    return d * lax.rsqrt(var + LN_EPS) * g + b


def _silu(x):
    return x * jax.nn.sigmoid(x)


def _gelu_tanh(x):
    c = math.sqrt(2.0 / math.pi)
    return 0.5 * x * (1.0 + jnp.tanh(c * (x + 0.044715 * (x * x * x))))


def _log_sigmoid(x):
    return jnp.minimum(x, 0.0) - jnp.log1p(jnp.exp(-jnp.abs(x)))


def _ffn_kernel(x_ref, wg_ref, wu_ref, wd_ref, g_ref, b_ref, o_ref, xb_ref, acc_ref, *, n_chunks):
    xb_ref[...] = x_ref[...].astype(BF16)
    acc_ref[...] = jnp.zeros_like(acc_ref)

    def body(c, carry):
        xb = xb_ref[...]
        gate = jnp.dot(xb, wg_ref[c], preferred_element_type=F32)
        up = jnp.dot(xb, wu_ref[c], preferred_element_type=F32)
        h = (_silu(gate) * up).astype(BF16)
        acc_ref[...] += jnp.dot(h, wd_ref[c], preferred_element_type=F32)
        return carry

    lax.fori_loop(0, n_chunks, body, 0)
    y = ALPHA * x_ref[...] + 0.5 * acc_ref[...]
    o_ref[...] = _layer_norm(y, g_ref[...], b_ref[...])


def _ffn(x, wg3, wu3, wd3, ln_g, ln_b):
    n, d = x.shape
    n_chunks, _, ck = wg3.shape
    tm = min(TM_FFN, n)
    return pl.pallas_call(
        functools.partial(_ffn_kernel, n_chunks=n_chunks),
        grid=(n // tm,),
        in_specs=[
            pl.BlockSpec((tm, d), lambda i: (i, 0)),
            _resident((n_chunks, d, ck)),
            _resident((n_chunks, d, ck)),
            _resident((n_chunks, ck, d)),
            _resident((1, d)),
            _resident((1, d)),
        ],
        out_specs=pl.BlockSpec((tm, d), lambda i: (i, 0)),
        out_shape=jax.ShapeDtypeStruct((n, d), F32),
        scratch_shapes=[pltpu.VMEM((tm, d), BF16), pltpu.VMEM((tm, d), F32)],
        compiler_params=_params(1),
        name="ffn_ln",
    )(x, wg3, wu3, wd3, ln_g, ln_b)


def _inproj_kernel(x_ref, w_ref, b_ref, lng_ref, lnb_ref, ws_ref, bs_ref, cos_ref, sin_ref,
                   gate_ref, ya_ref, q_ref, k_ref, v_ref, g_ref, zc_ref, xb_ref, *, tm):
    xb_ref[...] = x_ref[...].astype(BF16)

    def proj(lo, width):
        return (jnp.dot(xb_ref[...], w_ref[:, lo:lo + width], preferred_element_type=F32)
                + b_ref[:, lo:lo + width])

    for j in range(N_BRANCH):
        gate_ref[:, j * D_MODEL:(j + 1) * D_MODEL] = jax.nn.sigmoid(
            proj(OFF_GATE + j * D_MODEL, D_MODEL)).astype(BF16)

    za = _gelu_tanh(proj(OFF_ZA, 2 * D_A))
    u = za[:, :D_A]
    vn = _layer_norm(za[:, D_A:], lng_ref[...], lnb_ref[...]).astype(BF16)
    for r in range(tm // CHUNK):
        rows = slice(r * CHUNK, (r + 1) * CHUNK)
        for gi in range(G_A):
            cols = slice(gi * GA_DIM, (gi + 1) * GA_DIM)
            mixed = jnp.dot(ws_ref[gi], vn[rows, cols], preferred_element_type=F32) + bs_ref[gi]
            ya_ref[rows, cols] = (u[rows, cols] * mixed).astype(BF16)

    cos = cos_ref[...]
    sin = sin_ref[...]
    qk = proj(OFF_Q, 2 * D_B)
    for hd in range(H_B):
        cols = slice(hd * HB_DIM, (hd + 1) * HB_DIM)
        s = qk[:, hd * HB_DIM:(hd + 1) * HB_DIM]
        q_ref[:, cols] = ((s * cos + pltpu.roll(s, HB_DIM // 2, 1) * sin)
                          * (HB_DIM ** -0.5)).astype(BF16)
        s = qk[:, D_B + hd * HB_DIM:D_B + (hd + 1) * HB_DIM]
        k_ref[:, cols] = (s * cos + pltpu.roll(s, HB_DIM // 2, 1) * sin).astype(BF16)

    vg = proj(OFF_V, 2 * D_B)
    v_ref[...] = vg[:, :D_B].astype(BF16)
    g_ref[...] = _silu(vg[:, D_B:]).astype(BF16)
    zc_ref[...] = proj(OFF_ZC, D_C).astype(BF16)


def _inproj(x, seq_len, w_in, b_in, sgu_ln_g, sgu_ln_b, sgu_w, sgu_bb, cos2, sin2):
    n, d = x.shape
    tm = min(TM_INPROJ, seq_len)
    pos_blocks = seq_len // tm
    row = lambda w: pl.BlockSpec((tm, w), lambda i: (i, 0))
    pos = pl.BlockSpec((tm, HB_DIM), lambda i: (i % pos_blocks, 0))
    out = lambda w: jax.ShapeDtypeStruct((n, w), BF16)
    return pl.pallas_call(
        functools.partial(_inproj_kernel, tm=tm),
        grid=(n // tm,),
        in_specs=[
            row(d),
            _resident((d, D_IN)),
            _resident((1, D_IN)),
            _resident((1, D_A)),
            _resident((1, D_A)),
            _resident((G_A, CHUNK, CHUNK)),
            _resident((G_A, CHUNK, GA_DIM)),
            pos,
            pos,
        ],
        out_specs=[row(N_BRANCH * D_MODEL), row(D_A), row(D_B), row(D_B), row(D_B), row(D_B), row(D_C)],
        out_shape=[out(N_BRANCH * D_MODEL), out(D_A), out(D_B), out(D_B), out(D_B), out(D_B), out(D_C)],
        scratch_shapes=[pltpu.VMEM((tm, d), BF16)],
        compiler_params=_params(1),
        name="inproj",
    )(x, w_in, b_in, sgu_ln_g, sgu_ln_b, sgu_w, sgu_bb, cos2, sin2)


def _ret_kernel(logit_ref, q_ref, k_ref, v_ref, *rest, cr, backward):
    if backward:
        ofwd_ref, g_ref, gng_ref, gnb_ref, o_ref, state_ref, dmat_ref, kdec_ref, qdec_ref, cdec_ref = rest
    else:
        o_ref, state_ref, dmat_ref, kdec_ref, qdec_ref, cdec_ref = rest

    @pl.when(pl.program_id(1) == 0)
    def _():
        state_ref[...] = jnp.zeros_like(state_ref)
        ri = lax.broadcasted_iota(jnp.int32, (cr, cr), 0)
        ci = lax.broadcasted_iota(jnp.int32, (cr, cr), 1)
        rel = (ci - ri) if backward else (ri - ci)
        mask = (rel > 0) if backward else (rel >= 0)
        relf = jnp.where(mask, rel, 0).astype(F32)
        idx = lax.broadcasted_iota(jnp.int32, (cr, HB_DIM), 0).astype(F32)
        for h in range(H_B):
            lg = _log_sigmoid(jnp.full((cr, cr), logit_ref[h], F32))
            dmat_ref[h] = jnp.where(mask, jnp.exp(relf * lg), 0.0)
            lgv = _log_sigmoid(jnp.full((cr, HB_DIM), logit_ref[h], F32))
            if backward:
                kdec_ref[h] = jnp.exp(idx * lgv)
                qdec_ref[h] = jnp.exp((cr - idx) * lgv)
            else:
                kdec_ref[h] = jnp.exp((cr - 1.0 - idx) * lgv)
                qdec_ref[h] = jnp.exp((idx + 1.0) * lgv)
            lgs = _log_sigmoid(jnp.full((HB_DIM, HB_DIM), logit_ref[h], F32))
            cdec_ref[h] = jnp.exp(float(cr) * lgs)

    for h in range(H_B):
        cols = slice(h * HB_DIM, (h + 1) * HB_DIM)
        q = q_ref[:, cols]
        k = k_ref[:, cols]
        v = v_ref[:, cols]
        scores = lax.dot_general(q, k, (((1,), (1,)), ((), ())), preferred_element_type=F32)
        p = (scores * dmat_ref[h]).astype(BF16)
        intra = jnp.dot(p, v, preferred_element_type=F32)
        state = state_ref[h]
        qd = (q.astype(F32) * qdec_ref[h]).astype(BF16)
        cross = jnp.dot(qd, state.astype(BF16), preferred_element_type=F32)
        kd_t = (k.astype(F32) * kdec_ref[h]).T.astype(BF16)
        state_ref[h] = cdec_ref[h] * state + jnp.dot(kd_t, v, preferred_element_type=F32)
        o = intra + cross
        if backward:
            o = o + ofwd_ref[:, cols]
            mu = jnp.mean(o, axis=-1, keepdims=True)
            dlt = o - mu
            var = jnp.mean(dlt * dlt, axis=-1, keepdims=True)
            on = dlt * lax.rsqrt(var + LN_EPS) * gng_ref[:, cols] + gnb_ref[:, cols]
            o_ref[:, cols] = (g_ref[:, cols].astype(F32) * on).astype(BF16)
        else:
            o_ref[:, cols] = o


def _retention_sweep(logits, q, k, v, seq_len, extra, *, backward):
    n, _ = q.shape
    cr = min(RET_CHUNK, seq_len)
    nc = seq_len // cr
    batch = n // seq_len
    if backward:
        idx = lambda b, c: (b * nc + nc - 1 - c, 0)
    else:
        idx = lambda b, c: (b * nc + c, 0)
    blk = pl.BlockSpec((cr, D_B), idx)
    in_specs = [pl.BlockSpec(memory_space=pltpu.SMEM), blk, blk, blk]
    if backward:
        in_specs += [blk, blk, _resident((1, D_B)), _resident((1, D_B))]
    return pl.pallas_call(
        functools.partial(_ret_kernel, cr=cr, backward=backward),
        grid=(batch, nc),
        in_specs=in_specs,
        out_specs=blk,
        out_shape=jax.ShapeDtypeStruct((n, D_B), BF16 if backward else F32),
        scratch_shapes=[
            pltpu.VMEM((H_B, HB_DIM, HB_DIM), F32),
            pltpu.VMEM((H_B, cr, cr), F32),
            pltpu.VMEM((H_B, cr, HB_DIM), F32),
            pltpu.VMEM((H_B, cr, HB_DIM), F32),
            pltpu.VMEM((H_B, HB_DIM, HB_DIM), F32),
        ],
        compiler_params=_params(2),
        name="retention_bwd" if backward else "retention_fwd",
    )(logits, q, k, v, *extra)


def _fft1_kernel(f_ref, z_ref, t_ref):
    t_ref[...] = jnp.dot(f_ref[...], z_ref[...], preferred_element_type=F32).astype(BF16)


def _fft3_kernel(g_ref, t_ref, cs_ref, o_ref, *, k1b, l2):
    for j in range(k1b):
        t = jnp.concatenate([t_ref[0, j], t_ref[1, j]], axis=0)
        p = jnp.dot(g_ref[j], t, preferred_element_type=F32)
        for gi in range(G_C):
            cols = slice(gi * GC_DIM, (gi + 1) * GC_DIM)
            pc = jnp.concatenate([p[:l2, cols], p[l2:, cols]], axis=1).astype(BF16)
            y = jnp.dot(pc, cs_ref[...], preferred_element_type=F32)
            o_ref[:, j * D_C + gi * GC_DIM:j * D_C + (gi + 1) * GC_DIM] = y.astype(BF16)


def _fourier_tables(seq_len):
    l2 = min(FFT_L2, seq_len)
    l1 = seq_len // l2
    two_pi = 2.0 * math.pi
    i1 = jnp.arange(l1, dtype=jnp.int32)
    a1 = ((i1[:, None] * i1[None, :]) % l1).astype(F32) * (two_pi / l1)
    f1 = jnp.concatenate([jnp.cos(a1), -jnp.sin(a1)], axis=0) * (l1 ** -0.5)
    i2 = jnp.arange(l2, dtype=jnp.int32)
    kk = i1[:, None, None] + l1 * i2[None, :, None]
    a2 = ((kk * i2[None, None, :]) % seq_len).astype(F32) * (two_pi / seq_len)
    gr = jnp.cos(a2) * (l2 ** -0.5)
    gi = -jnp.sin(a2) * (l2 ** -0.5)
    g = jnp.concatenate([jnp.concatenate([gr, -gi], axis=2),
                         jnp.concatenate([gi, gr], axis=2)], axis=1)
    ic = jnp.arange(GC_DIM, dtype=jnp.int32)
    ac = ((ic[:, None] * ic[None, :]) % GC_DIM).astype(F32) * (two_pi / GC_DIM)
    cs = jnp.concatenate([jnp.cos(ac), jnp.sin(ac)], axis=0) * (GC_DIM ** -0.5)
    return f1.astype(BF16), g.astype(BF16), cs.astype(BF16)


def _fourier(zc, seq_len, tables):
    f1, g, cs = tables
    n, _ = zc.shape
    batch = n // seq_len
    l1 = g.shape[0]
    l2 = seq_len // l1
    width = l2 * D_C
    cb = min(FFT_COLS, width)
    t = pl.pallas_call(
        _fft1_kernel,
        grid=(batch, width // cb),
        in_specs=[_resident((2 * l1, l1)),
                  pl.BlockSpec((None, l1, cb), lambda b, c: (b, 0, c))],
        out_specs=pl.BlockSpec((None, 2 * l1, cb), lambda b, c: (b, 0, c)),
        out_shape=jax.ShapeDtypeStruct((batch, 2 * l1, width), BF16),
        compiler_params=_params(2),
        name="fourier_stage1",
    )(f1, zc.reshape(batch, l1, width))
    k1b = min(FFT_K1B, l1)
    y = pl.pallas_call(
        functools.partial(_fft3_kernel, k1b=k1b, l2=l2),
        grid=(batch, l1 // k1b),
        in_specs=[pl.BlockSpec((k1b, 2 * l2, 2 * l2), lambda b, c: (c, 0, 0)),
                  pl.BlockSpec((None, 2, k1b, l2, D_C), lambda b, c: (b, 0, c, 0, 0)),
                  _resident((2 * GC_DIM, GC_DIM))],
        out_specs=pl.BlockSpec((None, l2, k1b * D_C), lambda b, c: (b, 0, c)),
        out_shape=jax.ShapeDtypeStruct((batch, l2, l1 * D_C), BF16),
        compiler_params=_params(2),
        name="fourier_stage2",
    )(g, t.reshape(batch, 2, l1, l2, D_C), cs)
    return y.reshape(n, D_C)


def _combine_kernel(x_ref, gate_ref, ya_ref, yb_ref, yc_ref, wa_ref, wb_ref, wc_ref, wo_ref,
                    g_ref, b_ref, o_ref):
    s = None
    for j, (y_ref, w_ref) in enumerate(((ya_ref, wa_ref), (yb_ref, wb_ref), (yc_ref, wc_ref))):
        y = jnp.dot(y_ref[...], w_ref[...], preferred_element_type=F32)
        term = gate_ref[:, j * D_MODEL:(j + 1) * D_MODEL].astype(F32) * y
        s = term if s is None else s + term
    mix = jnp.dot(s.astype(BF16), wo_ref[...], preferred_element_type=F32)
    o_ref[...] = _layer_norm(ALPHA * x_ref[...] + mix, g_ref[...], b_ref[...])


def _combine(x, gates, ya, yb, yc, wa, wb, wc, wo, ln_g, ln_b):
    n, d = x.shape
    tm = min(TM_FFN, n)
    row = lambda w: pl.BlockSpec((tm, w), lambda i: (i, 0))
    return pl.pallas_call(
        _combine_kernel,
        grid=(n // tm,),
        in_specs=[row(d), row(N_BRANCH * d), row(D_A), row(D_B), row(D_C),
                  _resident((D_A, d)), _resident((D_B, d)), _resident((D_C, d)), _resident((d, d)),
                  _resident((1, d)), _resident((1, d))],
        out_specs=row(d),
        out_shape=jax.ShapeDtypeStruct((n, d), F32),
        compiler_params=_params(1),
        name="combine_ln",
    )(x, gates, ya, yb, yc, wa, wb, wc, wo, ln_g, ln_b)


def _rotary_tables(seq_len):
    half = HB_DIM // 2
    inv = ROPE_BASE ** (-jnp.arange(half, dtype=F32) / half)
    ang = jnp.arange(seq_len, dtype=F32)[:, None] * inv[None, :]
    cos = jnp.cos(ang)
    sin = jnp.sin(ang)
    return jnp.concatenate([cos, cos], axis=1), jnp.concatenate([-sin, sin], axis=1)


def _prep_layer(p, l):
    n_chunks = D_FF // FF_CHUNK
    cols = lambda w: w[l].astype(BF16).reshape(D_MODEL, n_chunks, FF_CHUNK).transpose(1, 0, 2)
    rows = lambda w: w[l].astype(BF16).reshape(n_chunks, FF_CHUNK, D_MODEL)
    vec = lambda w: w[l].reshape(1, -1).astype(F32)
    return dict(
        ffn1=(cols(p["ffn1_w_gate"]), cols(p["ffn1_w_up"]), rows(p["ffn1_w_down"]),
              vec(p["ln1_g"]), vec(p["ln1_b"])),
        ffn2=(cols(p["ffn2_w_gate"]), cols(p["ffn2_w_up"]), rows(p["ffn2_w_down"]),
              vec(p["ln3_g"]), vec(p["ln3_b"])),
        inproj=(p["w_in"][l].astype(BF16), vec(p["b_in"]), vec(p["sgu_ln_g"]), vec(p["sgu_ln_b"]),
                p["sgu_w"][l].astype(BF16),
                jnp.broadcast_to(p["sgu_b"][l][:, :, None], (G_A, CHUNK, GA_DIM)).astype(F32)),
        logit_f=p["ret_logit_fwd"][l].astype(F32),
        logit_b=p["ret_logit_bwd"][l].astype(F32),
        gn=(vec(p["ret_gn_g"]), vec(p["ret_gn_b"])),
        combine=(p["w_br_a"][l].astype(BF16), p["w_br_b"][l].astype(BF16), p["w_br_c"][l].astype(BF16),
                 p["w_o"][l].astype(BF16), vec(p["ln2_g"]), vec(p["ln2_b"])),
    )


def _trunk(x, layers):
    batch, seq_len, d = x.shape
    x = x.reshape(batch * seq_len, d)
    cos2, sin2 = _rotary_tables(seq_len)
    tables = _fourier_tables(seq_len)
    for lw in layers:
        x = _ffn(x, *lw["ffn1"])
        gates, ya, q, k, v, g, zc = _inproj(x, seq_len, *lw["inproj"], cos2, sin2)
        o_fwd = _retention_sweep(lw["logit_f"], q, k, v, seq_len, (), backward=False)
        yb = _retention_sweep(lw["logit_b"], q, k, v, seq_len, (o_fwd, g) + lw["gn"], backward=True)
        yc = _fourier(zc, seq_len, tables)
        x = _combine(x, gates, ya, yb, yc, *lw["combine"])
        x = _ffn(x, *lw["ffn2"])
    return x.reshape(batch, seq_len, d)


def kernel(x_prompt, x_sample, ffn1_w_gate, ffn1_w_up, ffn1_w_down, ln1_g, ln1_b, w_in, b_in, sgu_ln_g, sgu_ln_b, sgu_w, sgu_b, ret_logit_fwd, ret_logit_bwd, ret_gn_g, ret_gn_b, w_br_a, w_br_b, w_br_c, w_o, ln2_g, ln2_b, ffn2_w_gate, ffn2_w_up, ffn2_w_down, ln3_g, ln3_b):
    p = dict(ffn1_w_gate=ffn1_w_gate, ffn1_w_up=ffn1_w_up, ffn1_w_down=ffn1_w_down, ln1_g=ln1_g,
             ln1_b=ln1_b, w_in=w_in, b_in=b_in, sgu_ln_g=sgu_ln_g, sgu_ln_b=sgu_ln_b, sgu_w=sgu_w,
             sgu_b=sgu_b, ret_logit_fwd=ret_logit_fwd, ret_logit_bwd=ret_logit_bwd, ret_gn_g=ret_gn_g,
             ret_gn_b=ret_gn_b, w_br_a=w_br_a, w_br_b=w_br_b, w_br_c=w_br_c, w_o=w_o, ln2_g=ln2_g,
             ln2_b=ln2_b, ffn2_w_gate=ffn2_w_gate, ffn2_w_up=ffn2_w_up, ffn2_w_down=ffn2_w_down,
             ln3_g=ln3_g, ln3_b=ln3_b)
    layers = [_prep_layer(p, l) for l in range(DEPTH)]
    return (_trunk(x_prompt, layers), _trunk(x_sample, layers))
```

```python
import functools
import math

import jax
import jax.numpy as jnp
from jax import lax
from jax.experimental import pallas as pl
from jax.experimental.pallas import tpu as pltpu

F32 = jnp.float32
BF16 = jnp.bfloat16

D_MODEL = 1024
DEPTH = 2
CHUNK = 128
D_A = 512
G_A = 4
GA_DIM = D_A // G_A
D_B = 512
H_B = 4
HB_DIM = D_B // H_B
D_C = 512
G_C = 4
GC_DIM = D_C // G_C
N_BRANCH = 3
D_FF = 2816
ROPE_BASE = 10000.0
LN_EPS = 1e-5
ALPHA = (2.0 * DEPTH) ** 0.25
D_IN = N_BRANCH * D_MODEL + 2 * D_A + 4 * D_B + D_C

OFF_GATE = 0
OFF_ZA = N_BRANCH * D_MODEL
OFF_Q = OFF_ZA + 2 * D_A
OFF_K = OFF_Q + D_B
OFF_V = OFF_K + D_B
OFF_G = OFF_V + D_B
OFF_ZC = OFF_G + D_B

LANES = 128
FF_CHUNK = 256
TM_FFN = 512
TM_INPROJ = 512
RET_CHUNK = 512
FFT_L2 = 128
FFT_COLS = 4096
FFT_K1B = 8
VMEM_LIMIT = 56 * 1024 * 1024


def _params(n_axes):
    return pltpu.CompilerParams(dimension_semantics=("arbitrary",) * n_axes,
                                vmem_limit_bytes=VMEM_LIMIT)


def _resident(shape):
    zeros = (0,) * len(shape)
    return pl.BlockSpec(shape, lambda *_: zeros, pipeline_mode=pl.Buffered(1))


def _layer_norm(y, g, b):
    mu = jnp.mean(y, axis=-1, keepdims=True)
    d = y - mu
    var = jnp.mean(d * d, axis=-1, keepdims=True)
    return d * lax.rsqrt(var + LN_EPS) * g + b


def _silu(x):
    return x * jax.nn.sigmoid(x)


def _gelu_tanh(x):
    c = math.sqrt(2.0 / math.pi)
    return 0.5 * x * (1.0 + jnp.tanh(c * (x + 0.044715 * (x * x * x))))


def _log_sigmoid(x):
    return jnp.minimum(x, 0.0) - jnp.log1p(jnp.exp(-jnp.abs(x)))


def _ffn_kernel(x_ref, wg_ref, wu_ref, wd_ref, g_ref, b_ref, o_ref, xb_ref, acc_ref):
    xb_ref[...] = x_ref[...].astype(BF16)
    for c in range(D_FF // FF_CHUNK):
        cols = slice(c * FF_CHUNK, (c + 1) * FF_CHUNK)
        xb = xb_ref[...]
        gate = jnp.dot(xb, wg_ref[:, cols], preferred_element_type=F32)
        up = jnp.dot(xb, wu_ref[:, cols], preferred_element_type=F32)
        h = (_silu(gate) * up).astype(BF16)
        down = jnp.dot(h, wd_ref[cols, :], preferred_element_type=F32)
        if c == 0:
            acc_ref[...] = down
        else:
            acc_ref[...] += down
    y = ALPHA * x_ref[...] + 0.5 * acc_ref[...]
    o_ref[...] = _layer_norm(y, g_ref[...], b_ref[...])


def _ffn(x, wg, wu, wd, ln_g, ln_b):
    n, d = x.shape
    tm = min(TM_FFN, n)
    return pl.pallas_call(
        _ffn_kernel,
        grid=(n // tm,),
        in_specs=[
            pl.BlockSpec((tm, d), lambda i: (i, 0)),
            _resident((d, D_FF)),
            _resident((d, D_FF)),
            _resident((D_FF, d)),
            _resident((1, d)),
            _resident((1, d)),
        ],
        out_specs=pl.BlockSpec((tm, d), lambda i: (i, 0)),
        out_shape=jax.ShapeDtypeStruct((n, d), F32),
        scratch_shapes=[pltpu.VMEM((tm, d), BF16), pltpu.VMEM((tm, d), F32)],
        compiler_params=_params(1),
        name="ffn_ln",
    )(x, wg, wu, wd, ln_g, ln_b)


def _inproj_kernel(x_ref, w_ref, b_ref, lng_ref, lnb_ref, ws_ref, bs_ref, cos_ref, sin_ref,
                   gate_ref, ya_ref, q_ref, k_ref, v_ref, g_ref, zc_ref, xb_ref, *, tm):
    xb_ref[...] = x_ref[...].astype(BF16)

    def proj(lo, width):
        return (jnp.dot(xb_ref[...], w_ref[:, lo:lo + width], preferred_element_type=F32)
                + b_ref[:, lo:lo + width])

    for j in range(N_BRANCH):
        gate_ref[:, j * D_MODEL:(j + 1) * D_MODEL] = jax.nn.sigmoid(
            proj(OFF_GATE + j * D_MODEL, D_MODEL)).astype(BF16)

    za = _gelu_tanh(proj(OFF_ZA, 2 * D_A))
    u = za[:, :D_A]
    vn = _layer_norm(za[:, D_A:], lng_ref[...], lnb_ref[...]).astype(BF16)
    for r in range(tm // CHUNK):
        rows = slice(r * CHUNK, (r + 1) * CHUNK)
        for gi in range(G_A):
            cols = slice(gi * GA_DIM, (gi + 1) * GA_DIM)
            mixed = jnp.dot(ws_ref[gi], vn[rows, cols], preferred_element_type=F32) + bs_ref[gi]
            ya_ref[rows, cols] = (u[rows, cols] * mixed).astype(BF16)

    cos = cos_ref[...]
    sin = sin_ref[...]
    qk = proj(OFF_Q, 2 * D_B)
    for hd in range(H_B):
        cols = slice(hd * HB_DIM, (hd + 1) * HB_DIM)
        s = qk[:, hd * HB_DIM:(hd + 1) * HB_DIM]
        q_ref[:, cols] = ((s * cos + pltpu.roll(s, HB_DIM // 2, 1) * sin)
                          * (HB_DIM ** -0.5)).astype(BF16)
        s = qk[:, D_B + hd * HB_DIM:D_B + (hd + 1) * HB_DIM]
        k_ref[:, cols] = (s * cos + pltpu.roll(s, HB_DIM // 2, 1) * sin).astype(BF16)

    vg = proj(OFF_V, 2 * D_B)
    v_ref[...] = vg[:, :D_B].astype(BF16)
    g_ref[...] = _silu(vg[:, D_B:]).astype(BF16)
    zc_ref[...] = proj(OFF_ZC, D_C).astype(BF16)


def _inproj(x, seq_len, w_in, b_in, sgu_ln_g, sgu_ln_b, sgu_w, sgu_bb, cos2, sin2):
    n, d = x.shape
    tm = min(TM_INPROJ, seq_len)
    pos_blocks = seq_len // tm
    row = lambda w: pl.BlockSpec((tm, w), lambda i: (i, 0))
    pos = pl.BlockSpec((tm, HB_DIM), lambda i: (i % pos_blocks, 0))
    out = lambda w: jax.ShapeDtypeStruct((n, w), BF16)
    return pl.pallas_call(
        functools.partial(_inproj_kernel, tm=tm),
        grid=(n // tm,),
        in_specs=[
            row(d),
            _resident((d, D_IN)),
            _resident((1, D_IN)),
            _resident((1, D_A)),
            _resident((1, D_A)),
            _resident((G_A, CHUNK, CHUNK)),
            _resident((G_A, CHUNK, GA_DIM)),
            pos,
            pos,
        ],
        out_specs=[row(N_BRANCH * D_MODEL), row(D_A), row(D_B), row(D_B), row(D_B), row(D_B), row(D_C)],
        out_shape=[out(N_BRANCH * D_MODEL), out(D_A), out(D_B), out(D_B), out(D_B), out(D_B), out(D_C)],
        scratch_shapes=[pltpu.VMEM((tm, d), BF16)],
        compiler_params=_params(1),
        name="inproj",
    )(x, w_in, b_in, sgu_ln_g, sgu_ln_b, sgu_w, sgu_bb, cos2, sin2)


def _ret_kernel(logit_ref, q_ref, k_ref, v_ref, *rest, cr, backward):
    if backward:
        ofwd_ref, g_ref, gng_ref, gnb_ref, o_ref, state_ref, dmat_ref, kdec_ref, qdec_ref, cdec_ref = rest
    else:
        o_ref, state_ref, dmat_ref, kdec_ref, qdec_ref, cdec_ref = rest

    @pl.when(pl.program_id(1) == 0)
    def _():
        state_ref[...] = jnp.zeros_like(state_ref)
        ri = lax.broadcasted_iota(jnp.int32, (cr, cr), 0)
        ci = lax.broadcasted_iota(jnp.int32, (cr, cr), 1)
        rel = (ci - ri) if backward else (ri - ci)
        mask = (rel > 0) if backward else (rel >= 0)
        relf = jnp.where(mask, rel, 0).astype(F32)
        idx = lax.broadcasted_iota(jnp.int32, (cr, HB_DIM), 0).astype(F32)
        for h in range(H_B):
            lg = _log_sigmoid(jnp.full((cr, cr), logit_ref[h], F32))
            dmat_ref[h] = jnp.where(mask, jnp.exp(relf * lg), 0.0)
            lgv = _log_sigmoid(jnp.full((cr, HB_DIM), logit_ref[h], F32))
            if backward:
                kdec_ref[h] = jnp.exp(idx * lgv)
                qdec_ref[h] = jnp.exp((cr - idx) * lgv)
            else:
                kdec_ref[h] = jnp.exp((cr - 1.0 - idx) * lgv)
                qdec_ref[h] = jnp.exp((idx + 1.0) * lgv)
            lgs = _log_sigmoid(jnp.full((HB_DIM, HB_DIM), logit_ref[h], F32))
            cdec_ref[h] = jnp.exp(float(cr) * lgs)

    for h in range(H_B):
        cols = slice(h * HB_DIM, (h + 1) * HB_DIM)
        q = q_ref[:, cols]
        k = k_ref[:, cols]
        v = v_ref[:, cols]
        scores = lax.dot_general(q, k, (((1,), (1,)), ((), ())), preferred_element_type=F32)
        p = (scores * dmat_ref[h]).astype(BF16)
        intra = jnp.dot(p, v, preferred_element_type=F32)
        state = state_ref[h]
        qd = (q.astype(F32) * qdec_ref[h]).astype(BF16)
        cross = jnp.dot(qd, state.astype(BF16), preferred_element_type=F32)
        kd_t = (k.astype(F32) * kdec_ref[h]).T.astype(BF16)
        state_ref[h] = cdec_ref[h] * state + jnp.dot(kd_t, v, preferred_element_type=F32)
        o = intra + cross
        if backward:
            o = o + ofwd_ref[:, cols]
            mu = jnp.mean(o, axis=-1, keepdims=True)
            dlt = o - mu
            var = jnp.mean(dlt * dlt, axis=-1, keepdims=True)
            on = dlt * lax.rsqrt(var + LN_EPS) * gng_ref[:, cols] + gnb_ref[:, cols]
            o_ref[:, cols] = (g_ref[:, cols].astype(F32) * on).astype(BF16)
        else:
            o_ref[:, cols] = o


def _retention_sweep(logits, q, k, v, seq_len, extra, *, backward):
    n, _ = q.shape
    cr = min(RET_CHUNK, seq_len)
    nc = seq_len // cr
    batch = n // seq_len
    if backward:
        idx = lambda b, c: (b * nc + nc - 1 - c, 0)
    else:
        idx = lambda b, c: (b * nc + c, 0)
    blk = pl.BlockSpec((cr, D_B), idx)
    in_specs = [pl.BlockSpec(memory_space=pltpu.SMEM), blk, blk, blk]
    if backward:
        in_specs += [blk, blk, _resident((1, D_B)), _resident((1, D_B))]
    return pl.pallas_call(
        functools.partial(_ret_kernel, cr=cr, backward=backward),
        grid=(batch, nc),
        in_specs=in_specs,
        out_specs=blk,
        out_shape=jax.ShapeDtypeStruct((n, D_B), BF16 if backward else F32),
        scratch_shapes=[
            pltpu.VMEM((H_B, HB_DIM, HB_DIM), F32),
            pltpu.VMEM((H_B, cr, cr), F32),
            pltpu.VMEM((H_B, cr, HB_DIM), F32),
            pltpu.VMEM((H_B, cr, HB_DIM), F32),
            pltpu.VMEM((H_B, HB_DIM, HB_DIM), F32),
        ],
        compiler_params=_params(2),
        name="retention_bwd" if backward else "retention_fwd",
    )(logits, q, k, v, *extra)


def _fft1_kernel(f_ref, z_ref, t_ref):
    t_ref[...] = jnp.dot(f_ref[...], z_ref[...], preferred_element_type=F32).astype(BF16)


def _fft3_kernel(g_ref, t_ref, cs_ref, o_ref, *, k1b, l2):
    for j in range(k1b):
        t = jnp.concatenate([t_ref[0, j], t_ref[1, j]], axis=0)
        p = jnp.dot(g_ref[j], t, preferred_element_type=F32)
        for gi in range(G_C):
            cols = slice(gi * GC_DIM, (gi + 1) * GC_DIM)
            pc = jnp.concatenate([p[:l2, cols], p[l2:, cols]], axis=1).astype(BF16)
            y = jnp.dot(pc, cs_ref[...], preferred_element_type=F32)
            o_ref[:, j * D_C + gi * GC_DIM:j * D_C + (gi + 1) * GC_DIM] = y.astype(BF16)


def _fourier_tables(seq_len):
    l2 = min(FFT_L2, seq_len)
    l1 = seq_len // l2
    two_pi = 2.0 * math.pi
    i1 = jnp.arange(l1, dtype=jnp.int32)
    a1 = ((i1[:, None] * i1[None, :]) % l1).astype(F32) * (two_pi / l1)
    f1 = jnp.concatenate([jnp.cos(a1), -jnp.sin(a1)], axis=0) * (l1 ** -0.5)
    i2 = jnp.arange(l2, dtype=jnp.int32)
    kk = i1[:, None, None] + l1 * i2[None, :, None]
    a2 = ((kk * i2[None, None, :]) % seq_len).astype(F32) * (two_pi / seq_len)
    gr = jnp.cos(a2) * (l2 ** -0.5)
    gi = -jnp.sin(a2) * (l2 ** -0.5)
    g = jnp.concatenate([jnp.concatenate([gr, -gi], axis=2),
                         jnp.concatenate([gi, gr], axis=2)], axis=1)
    ic = jnp.arange(GC_DIM, dtype=jnp.int32)
    ac = ((ic[:, None] * ic[None, :]) % GC_DIM).astype(F32) * (two_pi / GC_DIM)
    cs = jnp.concatenate([jnp.cos(ac), jnp.sin(ac)], axis=0) * (GC_DIM ** -0.5)
    return f1.astype(BF16), g.astype(BF16), cs.astype(BF16)


def _fourier(zc, seq_len, tables):
    f1, g, cs = tables
    n, _ = zc.shape
    batch = n // seq_len
    l1 = g.shape[0]
    l2 = seq_len // l1
    width = l2 * D_C
    cb = min(FFT_COLS, width)
    t = pl.pallas_call(
        _fft1_kernel,
        grid=(batch, width // cb),
        in_specs=[_resident((2 * l1, l1)),
                  pl.BlockSpec((None, l1, cb), lambda b, c: (b, 0, c))],
        out_specs=pl.BlockSpec((None, 2 * l1, cb), lambda b, c: (b, 0, c)),
        out_shape=jax.ShapeDtypeStruct((batch, 2 * l1, width), BF16),
        compiler_params=_params(2),
        name="fourier_stage1",
    )(f1, zc.reshape(batch, l1, width))
    k1b = min(FFT_K1B, l1)
    y = pl.pallas_call(
        functools.partial(_fft3_kernel, k1b=k1b, l2=l2),
        grid=(batch, l1 // k1b),
        in_specs=[pl.BlockSpec((k1b, 2 * l2, 2 * l2), lambda b, c: (c, 0, 0)),
                  pl.BlockSpec((None, 2, k1b, l2, D_C), lambda b, c: (b, 0, c, 0, 0)),
                  _resident((2 * GC_DIM, GC_DIM))],
        out_specs=pl.BlockSpec((None, l2, k1b * D_C), lambda b, c: (b, 0, c)),
        out_shape=jax.ShapeDtypeStruct((batch, l2, l1 * D_C), BF16),
        compiler_params=_params(2),
        name="fourier_stage2",
    )(g, t.reshape(batch, 2, l1, l2, D_C), cs)
    return y.reshape(n, D_C)


def _combine_kernel(x_ref, gate_ref, ya_ref, yb_ref, yc_ref, wa_ref, wb_ref, wc_ref, wo_ref,
                    g_ref, b_ref, o_ref):
    s = None
    for j, (y_ref, w_ref) in enumerate(((ya_ref, wa_ref), (yb_ref, wb_ref), (yc_ref, wc_ref))):
        y = jnp.dot(y_ref[...], w_ref[...], preferred_element_type=F32)
        term = gate_ref[:, j * D_MODEL:(j + 1) * D_MODEL].astype(F32) * y
        s = term if s is None else s + term
    mix = jnp.dot(s.astype(BF16), wo_ref[...], preferred_element_type=F32)
    o_ref[...] = _layer_norm(ALPHA * x_ref[...] + mix, g_ref[...], b_ref[...])


def _combine(x, gates, ya, yb, yc, wa, wb, wc, wo, ln_g, ln_b):
    n, d = x.shape
    tm = min(TM_FFN, n)
    row = lambda w: pl.BlockSpec((tm, w), lambda i: (i, 0))
    return pl.pallas_call(
        _combine_kernel,
        grid=(n // tm,),
        in_specs=[row(d), row(N_BRANCH * d), row(D_A), row(D_B), row(D_C),
                  _resident((D_A, d)), _resident((D_B, d)), _resident((D_C, d)), _resident((d, d)),
                  _resident((1, d)), _resident((1, d))],
        out_specs=row(d),
        out_shape=jax.ShapeDtypeStruct((n, d), F32),
        compiler_params=_params(1),
        name="combine_ln",
    )(x, gates, ya, yb, yc, wa, wb, wc, wo, ln_g, ln_b)


def _rotary_tables(seq_len):
    half = HB_DIM // 2
    inv = ROPE_BASE ** (-jnp.arange(half, dtype=F32) / half)
    ang = jnp.arange(seq_len, dtype=F32)[:, None] * inv[None, :]
    cos = jnp.cos(ang)
    sin = jnp.sin(ang)
    return jnp.concatenate([cos, cos], axis=1), jnp.concatenate([-sin, sin], axis=1)


def _prep_layer(p, l):
    mat = lambda w: w[l].astype(BF16)
    vec = lambda w: w[l].reshape(1, -1).astype(F32)
    return dict(
        ffn1=(mat(p["ffn1_w_gate"]), mat(p["ffn1_w_up"]), mat(p["ffn1_w_down"]),
              vec(p["ln1_g"]), vec(p["ln1_b"])),
        ffn2=(mat(p["ffn2_w_gate"]), mat(p["ffn2_w_up"]), mat(p["ffn2_w_down"]),
              vec(p["ln3_g"]), vec(p["ln3_b"])),
        inproj=(p["w_in"][l].astype(BF16), vec(p["b_in"]), vec(p["sgu_ln_g"]), vec(p["sgu_ln_b"]),
                p["sgu_w"][l].astype(BF16),
                jnp.broadcast_to(p["sgu_b"][l][:, :, None], (G_A, CHUNK, GA_DIM)).astype(F32)),
        logit_f=p["ret_logit_fwd"][l].astype(F32),
        logit_b=p["ret_logit_bwd"][l].astype(F32),
        gn=(vec(p["ret_gn_g"]), vec(p["ret_gn_b"])),
        combine=(p["w_br_a"][l].astype(BF16), p["w_br_b"][l].astype(BF16), p["w_br_c"][l].astype(BF16),
                 p["w_o"][l].astype(BF16), vec(p["ln2_g"]), vec(p["ln2_b"])),
    )


def _trunk(x, layers):
    batch, seq_len, d = x.shape
    x = x.reshape(batch * seq_len, d)
    cos2, sin2 = _rotary_tables(seq_len)
    tables = _fourier_tables(seq_len)
    for lw in layers:
        x = _ffn(x, *lw["ffn1"])
        gates, ya, q, k, v, g, zc = _inproj(x, seq_len, *lw["inproj"], cos2, sin2)
        o_fwd = _retention_sweep(lw["logit_f"], q, k, v, seq_len, (), backward=False)
        yb = _retention_sweep(lw["logit_b"], q, k, v, seq_len, (o_fwd, g) + lw["gn"], backward=True)
        yc = _fourier(zc, seq_len, tables)
        x = _combine(x, gates, ya, yb, yc, *lw["combine"])
        x = _ffn(x, *lw["ffn2"])
    return x.reshape(batch, seq_len, d)


def kernel(x_prompt, x_sample, ffn1_w_gate, ffn1_w_up, ffn1_w_down, ln1_g, ln1_b, w_in, b_in, sgu_ln_g, sgu_ln_b, sgu_w, sgu_b, ret_logit_fwd, ret_logit_bwd, ret_gn_g, ret_gn_b, w_br_a, w_br_b, w_br_c, w_o, ln2_g, ln2_b, ffn2_w_gate, ffn2_w_up, ffn2_w_down, ln3_g, ln3_b):
    p = dict(ffn1_w_gate=ffn1_w_gate, ffn1_w_up=ffn1_w_up, ffn1_w_down=ffn1_w_down, ln1_g=ln1_g,
             ln1_b=ln1_b, w_in=w_in, b_in=b_in, sgu_ln_g=sgu_ln_g, sgu_ln_b=sgu_ln_b, sgu_w=sgu_w,
             sgu_b=sgu_b, ret_logit_fwd=ret_logit_fwd, ret_logit_bwd=ret_logit_bwd, ret_gn_g=ret_gn_g,
             ret_gn_b=ret_gn_b, w_br_a=w_br_a, w_br_b=w_br_b, w_br_c=w_br_c, w_o=w_o, ln2_g=ln2_g,
             ln2_b=ln2_b, ffn2_w_gate=ffn2_w_gate, ffn2_w_up=ffn2_w_up, ffn2_w_down=ffn2_w_down,
             ln3_g=ln3_g, ln3_b=ln3_b)
    layers = [_prep_layer(p, l) for l in range(DEPTH)]
    return (_trunk(x_prompt, layers), _trunk(x_sample, layers))
```

```python
import functools
import math

import jax
import jax.numpy as jnp
from jax import lax
from jax.experimental import pallas as pl
from jax.experimental.pallas import tpu as pltpu

F32 = jnp.float32
BF16 = jnp.bfloat16

D_MODEL = 1024
DEPTH = 2
CHUNK = 128
D_A = 512
G_A = 4
GA_DIM = D_A // G_A
D_B = 512
H_B = 4
HB_DIM = D_B // H_B
D_C = 512
G_C = 4
GC_DIM = D_C // G_C
N_BRANCH = 3
D_FF = 2816
ROPE_BASE = 10000.0
LN_EPS = 1e-5
ALPHA = (2.0 * DEPTH) ** 0.25
D_IN = N_BRANCH * D_MODEL + 2 * D_A + 4 * D_B + D_C

OFF_GATE = 0
OFF_ZA = N_BRANCH * D_MODEL
OFF_Q = OFF_ZA + 2 * D_A
OFF_K = OFF_Q + D_B
OFF_V = OFF_K + D_B
OFF_G = OFF_V + D_B
OFF_ZC = OFF_G + D_B

LANES = 128
FF_CHUNK = 256
TM_FFN = 512
TM_INPROJ = 512
RET_CHUNK = 512
FFT_L2 = 128
FFT_N2B = 8
FFT_K1B = 8
VMEM_LIMIT = 56 * 1024 * 1024


def _params(n_axes):
    return pltpu.CompilerParams(dimension_semantics=("arbitrary",) * n_axes,
                                vmem_limit_bytes=VMEM_LIMIT)


def _resident(shape):
    zeros = (0,) * len(shape)
    return pl.BlockSpec(shape, lambda *_: zeros, pipeline_mode=pl.Buffered(1))


def _layer_norm(y, g, b):
    mu = jnp.mean(y, axis=-1, keepdims=True)
    d = y - mu
    var = jnp.mean(d * d, axis=-1, keepdims=True)
    return d * lax.rsqrt(var + LN_EPS) * g + b


def _silu(x):
    return x * jax.nn.sigmoid(x)


def _gelu_tanh(x):
    c = math.sqrt(2.0 / math.pi)
    return 0.5 * x * (1.0 + jnp.tanh(c * (x + 0.044715 * (x * x * x))))


def _log_sigmoid(x):
    return jnp.minimum(x, 0.0) - jnp.log1p(jnp.exp(-jnp.abs(x)))


def _ffn_kernel(x_ref, wg_ref, wu_ref, wd_ref, g_ref, b_ref, o_ref, xb_ref, acc_ref):
    xb_ref[...] = x_ref[...].astype(BF16)
    n_chunks = D_FF // FF_CHUNK
    for c in range(n_chunks):
        cols = slice(c * FF_CHUNK, (c + 1) * FF_CHUNK)
        xb = xb_ref[...]
        gate = jnp.dot(xb, wg_ref[:, cols], preferred_element_type=F32)
        up = jnp.dot(xb, wu_ref[:, cols], preferred_element_type=F32)
        h = (_silu(gate) * up).astype(BF16)
        down = jnp.dot(h, wd_ref[cols, :], preferred_element_type=F32)
        if c == 0:
            acc_ref[...] = down
        elif c < n_chunks - 1:
            acc_ref[...] += down
        else:
            y = ALPHA * x_ref[...] + 0.5 * (acc_ref[...] + down)
            o_ref[...] = _layer_norm(y, g_ref[...], b_ref[...])


def _ffn(x, wg, wu, wd, ln_g, ln_b):
    n, d = x.shape
    tm = min(TM_FFN, n)
    return pl.pallas_call(
        _ffn_kernel,
        grid=(n // tm,),
        in_specs=[
            pl.BlockSpec((tm, d), lambda i: (i, 0)),
            _resident((d, D_FF)),
            _resident((d, D_FF)),
            _resident((D_FF, d)),
            _resident((1, d)),
            _resident((1, d)),
        ],
        out_specs=pl.BlockSpec((tm, d), lambda i: (i, 0)),
        out_shape=jax.ShapeDtypeStruct((n, d), F32),
        scratch_shapes=[pltpu.VMEM((tm, d), BF16), pltpu.VMEM((tm, d), F32)],
        compiler_params=_params(1),
        name="ffn_ln",
    )(x, wg, wu, wd, ln_g, ln_b)


def _inproj_kernel(x_ref, w_ref, b_ref, lng_ref, lnb_ref, ws_ref, bs_ref, cos_ref, sin_ref,
                   gate_ref, ya_ref, q_ref, k_ref, v_ref, g_ref, zc_ref, xb_ref, *, tm):
    xb_ref[...] = x_ref[...].astype(BF16)

    def proj(lo, width):
        return (jnp.dot(xb_ref[...], w_ref[:, lo:lo + width], preferred_element_type=F32)
                + b_ref[:, lo:lo + width])

    for j in range(N_BRANCH):
        gate_ref[:, j * D_MODEL:(j + 1) * D_MODEL] = jax.nn.sigmoid(
            proj(OFF_GATE + j * D_MODEL, D_MODEL)).astype(BF16)

    za = _gelu_tanh(proj(OFF_ZA, 2 * D_A))
    u = za[:, :D_A]
    vn = _layer_norm(za[:, D_A:], lng_ref[...], lnb_ref[...]).astype(BF16)
    for r in range(tm // CHUNK):
        rows = slice(r * CHUNK, (r + 1) * CHUNK)
        for gi in range(G_A):
            cols = slice(gi * GA_DIM, (gi + 1) * GA_DIM)
            mixed = jnp.dot(ws_ref[gi], vn[rows, cols], preferred_element_type=F32) + bs_ref[gi]
            ya_ref[rows, cols] = (u[rows, cols] * mixed).astype(BF16)

    cos = cos_ref[...]
    sin = sin_ref[...]
    qk = proj(OFF_Q, 2 * D_B)
    for hd in range(H_B):
        cols = slice(hd * HB_DIM, (hd + 1) * HB_DIM)
        s = qk[:, hd * HB_DIM:(hd + 1) * HB_DIM]
        q_ref[:, cols] = ((s * cos + pltpu.roll(s, HB_DIM // 2, 1) * sin)
                          * (HB_DIM ** -0.5)).astype(BF16)
        s = qk[:, D_B + hd * HB_DIM:D_B + (hd + 1) * HB_DIM]
        k_ref[:, cols] = (s * cos + pltpu.roll(s, HB_DIM // 2, 1) * sin).astype(BF16)

    vg = proj(OFF_V, 2 * D_B)
    v_ref[...] = vg[:, :D_B].astype(BF16)
    g_ref[...] = _silu(vg[:, D_B:]).astype(BF16)
    zc = proj(OFF_ZC, D_C)
    for gi in range(G_C):
        zc_ref[gi] = zc[:, gi * GC_DIM:(gi + 1) * GC_DIM]


def _inproj(x, seq_len, w_in, b_in, sgu_ln_g, sgu_ln_b, sgu_w, sgu_bb, cos2, sin2):
    n, d = x.shape
    tm = min(TM_INPROJ, seq_len)
    pos_blocks = seq_len // tm
    row = lambda w: pl.BlockSpec((tm, w), lambda i: (i, 0))
    pos = pl.BlockSpec((tm, HB_DIM), lambda i: (i % pos_blocks, 0))
    out = lambda w: jax.ShapeDtypeStruct((n, w), BF16)
    return pl.pallas_call(
        functools.partial(_inproj_kernel, tm=tm),
        grid=(n // tm,),
        in_specs=[
            row(d),
            _resident((d, D_IN)),
            _resident((1, D_IN)),
            _resident((1, D_A)),
            _resident((1, D_A)),
            _resident((G_A, CHUNK, CHUNK)),
            _resident((G_A, CHUNK, GA_DIM)),
            pos,
            pos,
        ],
        out_specs=[row(N_BRANCH * D_MODEL), row(D_A), row(D_B), row(D_B), row(D_B), row(D_B),
                   pl.BlockSpec((G_C, tm, GC_DIM), lambda i: (0, i, 0))],
        out_shape=[out(N_BRANCH * D_MODEL), out(D_A), out(D_B), out(D_B), out(D_B), out(D_B),
                   jax.ShapeDtypeStruct((G_C, n, GC_DIM), F32)],
        scratch_shapes=[pltpu.VMEM((tm, d), BF16)],
        compiler_params=_params(1),
        name="inproj",
    )(x, w_in, b_in, sgu_ln_g, sgu_ln_b, sgu_w, sgu_bb, cos2, sin2)


def _log_decay(logit, shape):
    return _log_sigmoid(jnp.full(shape, logit, F32))


def _ret_state_step(k_ref, v_ref, cols, state_ref, kdec_ref, cdec_ref, h):
    state = state_ref[h]
    kd_t = (k_ref[:, cols].astype(F32) * kdec_ref[h]).T.astype(BF16)
    state_ref[h] = cdec_ref[h] * state + jnp.dot(kd_t, v_ref[:, cols], preferred_element_type=F32)
    return state


def _ret_fwd_kernel(lf_ref, lb_ref, q_ref, k_ref, v_ref, o_ref,
                    state_ref, dmat_ref, kdec_ref, qdec_ref, cdec_ref, *, cr):
    @pl.when(pl.program_id(1) == 0)
    def _():
        state_ref[...] = jnp.zeros_like(state_ref)
        ri = lax.broadcasted_iota(jnp.int32, (cr, cr), 0)
        ci = lax.broadcasted_iota(jnp.int32, (cr, cr), 1)
        relf = (ri - ci).astype(F32)
        idx = lax.broadcasted_iota(jnp.int32, (cr, HB_DIM), 0).astype(F32)
        for h in range(H_B):
            low = jnp.exp(jnp.maximum(relf, 0.0) * _log_decay(lf_ref[h], (cr, cr)))
            up = jnp.exp(jnp.maximum(-relf, 0.0) * _log_decay(lb_ref[h], (cr, cr)))
            dmat_ref[h] = jnp.where(ri >= ci, low, up).astype(BF16)
            lgv = _log_decay(lf_ref[h], (cr, HB_DIM))
            kdec_ref[h] = jnp.exp((cr - 1.0 - idx) * lgv)
            qdec_ref[h] = jnp.exp((idx + 1.0) * lgv).astype(BF16)
            cdec_ref[h] = jnp.exp(float(cr) * _log_decay(lf_ref[h], (HB_DIM, HB_DIM)))

    for h in range(H_B):
        cols = slice(h * HB_DIM, (h + 1) * HB_DIM)
        q = q_ref[:, cols]
        scores = lax.dot_general(q, k_ref[:, cols], (((1,), (1,)), ((), ())),
                                 preferred_element_type=F32)
        p = scores.astype(BF16) * dmat_ref[h]
        intra = jnp.dot(p, v_ref[:, cols], preferred_element_type=F32)
        state = _ret_state_step(k_ref, v_ref, cols, state_ref, kdec_ref, cdec_ref, h)
        cross = jnp.dot(q * qdec_ref[h], state.astype(BF16), preferred_element_type=F32)
        o_ref[:, cols] = intra + cross


def _ret_bwd_kernel(lb_ref, q_ref, k_ref, v_ref, o1_ref, g_ref, gng_ref, gnb_ref, o_ref,
                    state_ref, kdec_ref, qdec_ref, cdec_ref, *, cr):
    @pl.when(pl.program_id(1) == 0)
    def _():
        state_ref[...] = jnp.zeros_like(state_ref)
        idx = lax.broadcasted_iota(jnp.int32, (cr, HB_DIM), 0).astype(F32)
        for h in range(H_B):
            lgv = _log_decay(lb_ref[h], (cr, HB_DIM))
            kdec_ref[h] = jnp.exp(idx * lgv)
            qdec_ref[h] = jnp.exp((cr - idx) * lgv).astype(BF16)
            cdec_ref[h] = jnp.exp(float(cr) * _log_decay(lb_ref[h], (HB_DIM, HB_DIM)))

    for h in range(H_B):
        cols = slice(h * HB_DIM, (h + 1) * HB_DIM)
        state = _ret_state_step(k_ref, v_ref, cols, state_ref, kdec_ref, cdec_ref, h)
        cross = jnp.dot(q_ref[:, cols] * qdec_ref[h], state.astype(BF16),
                        preferred_element_type=F32)
        o = o1_ref[:, cols] + cross
        mu = jnp.mean(o, axis=-1, keepdims=True)
        dlt = o - mu
        var = jnp.mean(dlt * dlt, axis=-1, keepdims=True)
        on = dlt * lax.rsqrt(var + LN_EPS) * gng_ref[:, cols] + gnb_ref[:, cols]
        o_ref[:, cols] = (g_ref[:, cols].astype(F32) * on).astype(BF16)


def _retention(logit_f, logit_b, q, k, v, g, gn_g, gn_b, seq_len):
    n, _ = q.shape
    cr = min(RET_CHUNK, seq_len)
    nc = seq_len // cr
    batch = n // seq_len
    smem = pl.BlockSpec(memory_space=pltpu.SMEM)
    state = pltpu.VMEM((H_B, HB_DIM, HB_DIM), F32)
    kdec = pltpu.VMEM((H_B, cr, HB_DIM), F32)
    qdec = pltpu.VMEM((H_B, cr, HB_DIM), BF16)
    up = pl.BlockSpec((cr, D_B), lambda b, c: (b * nc + c, 0))
    o1 = pl.pallas_call(
        functools.partial(_ret_fwd_kernel, cr=cr),
        grid=(batch, nc),
        in_specs=[smem, smem, up, up, up],
        out_specs=up,
        out_shape=jax.ShapeDtypeStruct((n, D_B), F32),
        scratch_shapes=[state, pltpu.VMEM((H_B, cr, cr), BF16), kdec, qdec, state],
        compiler_params=_params(2),
        name="retention_fwd",
    )(logit_f, logit_b, q, k, v)
    down = pl.BlockSpec((cr, D_B), lambda b, c: (b * nc + nc - 1 - c, 0))
    return pl.pallas_call(
        functools.partial(_ret_bwd_kernel, cr=cr),
        grid=(batch, nc),
        in_specs=[smem, down, down, down, down, down, _resident((1, D_B)), _resident((1, D_B))],
        out_specs=down,
        out_shape=jax.ShapeDtypeStruct((n, D_B), BF16),
        scratch_shapes=[state, kdec, qdec, state],
        compiler_params=_params(2),
        name="retention_bwd",
    )(logit_b, q, k, v, o1, g, gn_g, gn_b)


def _fft1_kernel(f_ref, z_ref, t_ref, zs_ref, ts_ref, *, nb, l1):
    for gi in range(G_C):
        zs_ref[gi] = z_ref[gi].reshape(l1 * nb, GC_DIM)
    for j in range(nb):
        zj = jnp.concatenate([zs_ref[gi, pl.ds(j, l1, stride=nb), :] for gi in range(G_C)], axis=1)
        t = jnp.dot(f_ref[...], zj.astype(BF16), preferred_element_type=F32)
        for gi in range(G_C):
            ts_ref[gi, pl.ds(j, 2 * l1, stride=nb), :] = t[:, gi * GC_DIM:(gi + 1) * GC_DIM]
    for gi in range(G_C):
        t_ref[gi] = ts_ref[gi].reshape(2 * l1, nb, GC_DIM)


def _fft3_kernel(g_ref, t_ref, cs_ref, o_ref, os_ref, *, k1b, l2):
    for j in range(k1b):
        t = jnp.concatenate(
            [jnp.concatenate([t_ref[gi, ri, j] for gi in range(G_C)], axis=1) for ri in range(2)],
            axis=0).astype(BF16)
        p = jnp.dot(g_ref[j], t, preferred_element_type=F32)
        for gi in range(G_C):
            cols = slice(gi * GC_DIM, (gi + 1) * GC_DIM)
            pc = jnp.concatenate([p[:l2, cols], p[l2:, cols]], axis=1).astype(BF16)
            os_ref[gi, pl.ds(j, l2, stride=k1b), :] = jnp.dot(pc, cs_ref[...], preferred_element_type=F32)
    for gi in range(G_C):
        o_ref[gi] = os_ref[gi].reshape(l2, k1b, GC_DIM)


def _fourier_tables(seq_len):
    l2 = min(FFT_L2, seq_len)
    l1 = seq_len // l2
    two_pi = 2.0 * math.pi
    i1 = jnp.arange(l1, dtype=jnp.int32)
    a1 = ((i1[:, None] * i1[None, :]) % l1).astype(F32) * (two_pi / l1)
    f1 = jnp.concatenate([jnp.cos(a1), -jnp.sin(a1)], axis=0) * (l1 ** -0.5)
    i2 = jnp.arange(l2, dtype=jnp.int32)
    kk = i1[:, None, None] + l1 * i2[None, :, None]
    a2 = ((kk * i2[None, None, :]) % seq_len).astype(F32) * (two_pi / seq_len)
    gr = jnp.cos(a2) * (l2 ** -0.5)
    gi = -jnp.sin(a2) * (l2 ** -0.5)
    g = jnp.concatenate([jnp.concatenate([gr, -gi], axis=2),
                         jnp.concatenate([gi, gr], axis=2)], axis=1)
    ic = jnp.arange(GC_DIM, dtype=jnp.int32)
    ac = ((ic[:, None] * ic[None, :]) % GC_DIM).astype(F32) * (two_pi / GC_DIM)
    cs = jnp.concatenate([jnp.cos(ac), jnp.sin(ac)], axis=0) * (GC_DIM ** -0.5)
    return f1.astype(BF16), g.astype(BF16), cs.astype(BF16)


def _fourier(zc, seq_len, tables):
    f1, g, cs = tables
    _, n, _ = zc.shape
    batch = n // seq_len
    l1 = g.shape[0]
    l2 = seq_len // l1
    nb = FFT_N2B
    t = pl.pallas_call(
        functools.partial(_fft1_kernel, nb=nb, l1=l1),
        grid=(batch, l2 // nb),
        in_specs=[_resident((2 * l1, l1)),
                  pl.BlockSpec((G_C, None, l1, nb, GC_DIM), lambda b, c: (0, b, 0, c, 0))],
        out_specs=pl.BlockSpec((G_C, None, 2 * l1, nb, GC_DIM), lambda b, c: (0, b, 0, c, 0)),
        out_shape=jax.ShapeDtypeStruct((G_C, batch, 2 * l1, l2, GC_DIM), F32),
        scratch_shapes=[pltpu.VMEM((G_C, l1 * nb, GC_DIM), F32),
                        pltpu.VMEM((G_C, 2 * l1 * nb, GC_DIM), F32)],
        compiler_params=_params(2),
        name="fourier_stage1",
    )(f1, zc.reshape(G_C, batch, l1, l2, GC_DIM))
    k1b = min(FFT_K1B, l1)
    y = pl.pallas_call(
        functools.partial(_fft3_kernel, k1b=k1b, l2=l2),
        grid=(batch, l1 // k1b),
        in_specs=[pl.BlockSpec((k1b, 2 * l2, 2 * l2), lambda b, c: (c, 0, 0)),
                  pl.BlockSpec((G_C, None, 2, k1b, l2, GC_DIM), lambda b, c: (0, b, 0, c, 0, 0)),
                  _resident((2 * GC_DIM, GC_DIM))],
        out_specs=pl.BlockSpec((G_C, None, l2, k1b, GC_DIM), lambda b, c: (0, b, 0, c, 0)),
        out_shape=jax.ShapeDtypeStruct((G_C, batch, l2, l1, GC_DIM), F32),
        scratch_shapes=[pltpu.VMEM((G_C, l2 * k1b, GC_DIM), F32)],
        compiler_params=_params(2),
        name="fourier_stage2",
    )(g, t.reshape(G_C, batch, 2, l1, l2, GC_DIM), cs)
    return y.reshape(G_C, n, GC_DIM)


def _combine_kernel(x_ref, gate_ref, ya_ref, yb_ref, yc_ref, wa_ref, wb_ref, wc_ref, wo_ref,
                    g_ref, b_ref, o_ref):
    yc = jnp.concatenate([yc_ref[gi] for gi in range(G_C)], axis=1).astype(BF16)
    s = None
    for j, (br, w_ref) in enumerate(((ya_ref[...], wa_ref), (yb_ref[...], wb_ref), (yc, wc_ref))):
        y = jnp.dot(br, w_ref[...], preferred_element_type=F32)
        term = gate_ref[:, j * D_MODEL:(j + 1) * D_MODEL].astype(F32) * y
        s = term if s is None else s + term
    mix = jnp.dot(s.astype(BF16), wo_ref[...], preferred_element_type=F32)
    o_ref[...] = _layer_norm(ALPHA * x_ref[...] + mix, g_ref[...], b_ref[...])


def _combine(x, gates, ya, yb, yc, wa, wb, wc, wo, ln_g, ln_b):
    n, d = x.shape
    tm = min(TM_FFN, n)
    row = lambda w: pl.BlockSpec((tm, w), lambda i: (i, 0))
    return pl.pallas_call(
        _combine_kernel,
        grid=(n // tm,),
        in_specs=[row(d), row(N_BRANCH * d), row(D_A), row(D_B),
                  pl.BlockSpec((G_C, tm, GC_DIM), lambda i: (0, i, 0)),
                  _resident((D_A, d)), _resident((D_B, d)), _resident((D_C, d)), _resident((d, d)),
                  _resident((1, d)), _resident((1, d))],
        out_specs=row(d),
        out_shape=jax.ShapeDtypeStruct((n, d), F32),
        compiler_params=_params(1),
        name="combine_ln",
    )(x, gates, ya, yb, yc, wa, wb, wc, wo, ln_g, ln_b)


def _rotary_tables(seq_len):
    half = HB_DIM // 2
    inv = ROPE_BASE ** (-jnp.arange(half, dtype=F32) / half)
    ang = jnp.arange(seq_len, dtype=F32)[:, None] * inv[None, :]
    cos = jnp.cos(ang)
    sin = jnp.sin(ang)
    return jnp.concatenate([cos, cos], axis=1), jnp.concatenate([-sin, sin], axis=1)


def _prep_layer(p, l):
    mat = lambda w: w[l].astype(BF16)
    vec = lambda w: w[l].reshape(1, -1).astype(F32)
    return dict(
        ffn1=(mat(p["ffn1_w_gate"]), mat(p["ffn1_w_up"]), mat(p["ffn1_w_down"]),
              vec(p["ln1_g"]), vec(p["ln1_b"])),
        ffn2=(mat(p["ffn2_w_gate"]), mat(p["ffn2_w_up"]), mat(p["ffn2_w_down"]),
              vec(p["ln3_g"]), vec(p["ln3_b"])),
        inproj=(p["w_in"][l].astype(BF16), vec(p["b_in"]), vec(p["sgu_ln_g"]), vec(p["sgu_ln_b"]),
                p["sgu_w"][l].astype(BF16),
                jnp.broadcast_to(p["sgu_b"][l][:, :, None], (G_A, CHUNK, GA_DIM)).astype(F32)),
        logit_f=p["ret_logit_fwd"][l].astype(F32),
        logit_b=p["ret_logit_bwd"][l].astype(F32),
        gn=(vec(p["ret_gn_g"]), vec(p["ret_gn_b"])),
        combine=(p["w_br_a"][l].astype(BF16), p["w_br_b"][l].astype(BF16), p["w_br_c"][l].astype(BF16),
                 p["w_o"][l].astype(BF16), vec(p["ln2_g"]), vec(p["ln2_b"])),
    )


def _trunk(x, layers):
    batch, seq_len, d = x.shape
    x = x.reshape(batch * seq_len, d)
    cos2, sin2 = _rotary_tables(seq_len)
    tables = _fourier_tables(seq_len)
    for lw in layers:
        x = _ffn(x, *lw["ffn1"])
        gates, ya, q, k, v, g, zc = _inproj(x, seq_len, *lw["inproj"], cos2, sin2)
        yb = _retention(lw["logit_f"], lw["logit_b"], q, k, v, g, *lw["gn"], seq_len)
        yc = _fourier(zc, seq_len, tables)
        x = _combine(x, gates, ya, yb, yc, *lw["combine"])
        x = _ffn(x, *lw["ffn2"])
    return x.reshape(batch, seq_len, d)


def kernel(x_prompt, x_sample, ffn1_w_gate, ffn1_w_up, ffn1_w_down, ln1_g, ln1_b, w_in, b_in, sgu_ln_g, sgu_ln_b, sgu_w, sgu_b, ret_logit_fwd, ret_logit_bwd, ret_gn_g, ret_gn_b, w_br_a, w_br_b, w_br_c, w_o, ln2_g, ln2_b, ffn2_w_gate, ffn2_w_up, ffn2_w_down, ln3_g, ln3_b):
    p = dict(ffn1_w_gate=ffn1_w_gate, ffn1_w_up=ffn1_w_up, ffn1_w_down=ffn1_w_down, ln1_g=ln1_g,
             ln1_b=ln1_b, w_in=w_in, b_in=b_in, sgu_ln_g=sgu_ln_g, sgu_ln_b=sgu_ln_b, sgu_w=sgu_w,
             sgu_b=sgu_b, ret_logit_fwd=ret_logit_fwd, ret_logit_bwd=ret_logit_bwd, ret_gn_g=ret_gn_g,
             ret_gn_b=ret_gn_b, w_br_a=w_br_a, w_br_b=w_br_b, w_br_c=w_br_c, w_o=w_o, ln2_g=ln2_g,
             ln2_b=ln2_b, ffn2_w_gate=ffn2_w_gate, ffn2_w_up=ffn2_w_up, ffn2_w_down=ffn2_w_down,
             ln3_g=ln3_g, ln3_b=ln3_b)
    layers = [_prep_layer(p, l) for l in range(DEPTH)]
    return (_trunk(x_prompt, layers), _trunk(x_sample, layers))
```

```python
import functools
import math

import jax
import jax.numpy as jnp
from jax import lax
from jax.experimental import pallas as pl
from jax.experimental.pallas import tpu as pltpu

F32 = jnp.float32
BF16 = jnp.bfloat16

D_MODEL = 1024
DEPTH = 2
CHUNK = 128
D_A = 512
G_A = 4
GA_DIM = D_A // G_A
D_B = 512
H_B = 4
HB_DIM = D_B // H_B
D_C = 512
G_C = 4
GC_DIM = D_C // G_C
N_BRANCH = 3
D_FF = 2816
ROPE_BASE = 10000.0
LN_EPS = 1e-5
ALPHA = (2.0 * DEPTH) ** 0.25
D_IN = N_BRANCH * D_MODEL + 2 * D_A + 4 * D_B + D_C

OFF_GATE = 0
OFF_ZA = N_BRANCH * D_MODEL
OFF_Q = OFF_ZA + 2 * D_A
OFF_K = OFF_Q + D_B
OFF_V = OFF_K + D_B
OFF_G = OFF_V + D_B
OFF_ZC = OFF_G + D_B

LANES = 128
FF_CHUNK = 256
TM_FFN = 512
RET_CHUNK = 512
FFT_L2 = 128
FFT_N2B = 8
FFT_K1B = 8
VMEM_LIMIT = 56 * 1024 * 1024


def _params(n_axes):
    return pltpu.CompilerParams(dimension_semantics=("arbitrary",) * n_axes,
                                vmem_limit_bytes=VMEM_LIMIT)


def _resident(shape):
    zeros = (0,) * len(shape)
    return pl.BlockSpec(shape, lambda *_: zeros, pipeline_mode=pl.Buffered(1))


def _layer_norm(y, g, b):
    mu = jnp.mean(y, axis=-1, keepdims=True)
    d = y - mu
    var = jnp.mean(d * d, axis=-1, keepdims=True)
    return d * lax.rsqrt(var + LN_EPS) * g + b


def _silu(x):
    return x * jax.nn.sigmoid(x)


def _gelu_tanh(x):
    c = math.sqrt(2.0 / math.pi)
    return 0.5 * x * (1.0 + jnp.tanh(c * (x + 0.044715 * (x * x * x))))


def _log_sigmoid(x):
    return jnp.minimum(x, 0.0) - jnp.log1p(jnp.exp(-jnp.abs(x)))


def _ffn_kernel(x_ref, wg_ref, wu_ref, wd_ref, g_ref, b_ref, o_ref, xb_ref, acc_ref):
    xb_ref[...] = x_ref[...].astype(BF16)
    n_chunks = D_FF // FF_CHUNK
    for c in range(n_chunks):
        cols = slice(c * FF_CHUNK, (c + 1) * FF_CHUNK)
        xb = xb_ref[...]
        gate = jnp.dot(xb, wg_ref[:, cols], preferred_element_type=F32)
        up = jnp.dot(xb, wu_ref[:, cols], preferred_element_type=F32)
        h = (_silu(gate) * up).astype(BF16)
        down = jnp.dot(h, wd_ref[cols, :], preferred_element_type=F32)
        if c == 0:
            acc_ref[...] = down
        elif c < n_chunks - 1:
            acc_ref[...] += down
        else:
            y = ALPHA * x_ref[...] + 0.5 * (acc_ref[...] + down)
            o_ref[...] = _layer_norm(y, g_ref[...], b_ref[...])


def _ffn(x, wg, wu, wd, ln_g, ln_b):
    n, d = x.shape
    tm = min(TM_FFN, n)
    return pl.pallas_call(
        _ffn_kernel,
        grid=(n // tm,),
        in_specs=[
            pl.BlockSpec((tm, d), lambda i: (i, 0)),
            _resident((d, D_FF)),
            _resident((d, D_FF)),
            _resident((D_FF, d)),
            _resident((1, d)),
            _resident((1, d)),
        ],
        out_specs=pl.BlockSpec((tm, d), lambda i: (i, 0)),
        out_shape=jax.ShapeDtypeStruct((n, d), F32),
        scratch_shapes=[pltpu.VMEM((tm, d), BF16), pltpu.VMEM((tm, d), F32)],
        compiler_params=_params(1),
        name="ffn_ln",
    )(x, wg, wu, wd, ln_g, ln_b)


def _inproj_kernel(lb_ref, x_ref, w_ref, b_ref, lng_ref, lnb_ref, ws_ref, bs_ref, cos_ref, sin_ref,
                   gate_ref, ya_ref, q_ref, k_ref, v_ref, g_ref, zc_ref, sb_ref,
                   xb_ref, state_ref, kdec_ref, cdec_ref, *, tm):
    @pl.when(pl.program_id(1) == 0)
    def _():
        state_ref[...] = jnp.zeros_like(state_ref)
        idx = lax.broadcasted_iota(jnp.int32, (tm, HB_DIM), 0).astype(F32)
        for h in range(H_B):
            kdec_ref[h] = jnp.exp(idx * _log_decay(lb_ref[h], (tm, HB_DIM)))
            cdec_ref[h] = jnp.exp(float(tm) * _log_decay(lb_ref[h], (HB_DIM, HB_DIM)))

    xb_ref[...] = x_ref[...].astype(BF16)

    def proj(lo, width):
        return (jnp.dot(xb_ref[...], w_ref[:, lo:lo + width], preferred_element_type=F32)
                + b_ref[:, lo:lo + width])

    for j in range(N_BRANCH):
        gate_ref[:, j * D_MODEL:(j + 1) * D_MODEL] = jax.nn.sigmoid(
            proj(OFF_GATE + j * D_MODEL, D_MODEL)).astype(BF16)

    za = _gelu_tanh(proj(OFF_ZA, 2 * D_A))
    u = za[:, :D_A]
    vn = _layer_norm(za[:, D_A:], lng_ref[...], lnb_ref[...]).astype(BF16)
    for r in range(tm // CHUNK):
        rows = slice(r * CHUNK, (r + 1) * CHUNK)
        for gi in range(G_A):
            cols = slice(gi * GA_DIM, (gi + 1) * GA_DIM)
            mixed = jnp.dot(ws_ref[gi], vn[rows, cols], preferred_element_type=F32) + bs_ref[gi]
            ya_ref[rows, cols] = (u[rows, cols] * mixed).astype(BF16)

    cos = cos_ref[...]
    sin = sin_ref[...]
    qk = proj(OFF_Q, 2 * D_B)
    vg = proj(OFF_V, 2 * D_B)
    v = vg[:, :D_B].astype(BF16)
    v_ref[...] = v
    g_ref[...] = _silu(vg[:, D_B:]).astype(BF16)
    for hd in range(H_B):
        cols = slice(hd * HB_DIM, (hd + 1) * HB_DIM)
        s = qk[:, hd * HB_DIM:(hd + 1) * HB_DIM]
        q_ref[:, cols] = ((s * cos + pltpu.roll(s, HB_DIM // 2, 1) * sin)
                          * (HB_DIM ** -0.5)).astype(BF16)
        s = qk[:, D_B + hd * HB_DIM:D_B + (hd + 1) * HB_DIM]
        k = (s * cos + pltpu.roll(s, HB_DIM // 2, 1) * sin).astype(BF16)
        k_ref[:, cols] = k
        state = state_ref[hd]
        sb_ref[0, hd] = state.astype(BF16)
        kd_t = (k.astype(F32) * kdec_ref[hd]).T.astype(BF16)
        state_ref[hd] = cdec_ref[hd] * state + jnp.dot(kd_t, v[:, cols], preferred_element_type=F32)

    zc = proj(OFF_ZC, D_C)
    n2_blocks = CHUNK // FFT_N2B
    for gi in range(G_C):
        for r in range(tm // CHUNK):
            zc_ref[gi, :, r] = zc[r * CHUNK:(r + 1) * CHUNK, gi * GC_DIM:(gi + 1) * GC_DIM].reshape(
                n2_blocks, FFT_N2B, GC_DIM)


def _inproj(x, seq_len, logit_b, w_in, b_in, sgu_ln_g, sgu_ln_b, sgu_w, sgu_bb, cos2, sin2):
    n, d = x.shape
    tm = min(RET_CHUNK, seq_len)
    tps = seq_len // tm
    batch = n // seq_len
    l1 = seq_len // CHUNK
    n2_blocks = CHUNK // FFT_N2B
    row = lambda w: pl.BlockSpec((tm, w), lambda b, j: (b * tps + tps - 1 - j, 0))
    pos = pl.BlockSpec((tm, HB_DIM), lambda b, j: (tps - 1 - j, 0))
    out = lambda w: jax.ShapeDtypeStruct((n, w), BF16)
    return pl.pallas_call(
        functools.partial(_inproj_kernel, tm=tm),
        grid=(batch, tps),
        in_specs=[
            pl.BlockSpec(memory_space=pltpu.SMEM),
            row(d),
            _resident((d, D_IN)),
            _resident((1, D_IN)),
            _resident((1, D_A)),
            _resident((1, D_A)),
            _resident((G_A, CHUNK, CHUNK)),
            _resident((G_A, CHUNK, GA_DIM)),
            pos,
            pos,
        ],
        out_specs=[row(N_BRANCH * D_MODEL), row(D_A), row(D_B), row(D_B), row(D_B), row(D_B),
                   pl.BlockSpec((G_C, None, n2_blocks, tm // CHUNK, FFT_N2B, GC_DIM),
                                lambda b, j: (0, b, 0, tps - 1 - j, 0, 0)),
                   pl.BlockSpec((1, H_B, HB_DIM, HB_DIM), lambda b, j: (b * tps + tps - 1 - j, 0, 0, 0))],
        out_shape=[out(N_BRANCH * D_MODEL), out(D_A), out(D_B), out(D_B), out(D_B), out(D_B),
                   jax.ShapeDtypeStruct((G_C, batch, n2_blocks, l1, FFT_N2B, GC_DIM), F32),
                   jax.ShapeDtypeStruct((n // tm, H_B, HB_DIM, HB_DIM), BF16)],
        scratch_shapes=[pltpu.VMEM((tm, d), BF16),
                        pltpu.VMEM((H_B, HB_DIM, HB_DIM), F32),
                        pltpu.VMEM((H_B, tm, HB_DIM), F32),
                        pltpu.VMEM((H_B, HB_DIM, HB_DIM), F32)],
        compiler_params=_params(2),
        name="inproj",
    )(logit_b, x, w_in, b_in, sgu_ln_g, sgu_ln_b, sgu_w, sgu_bb, cos2, sin2)


def _log_decay(logit, shape):
    return _log_sigmoid(jnp.full(shape, logit, F32))


def _ret_kernel(lf_ref, lb_ref, q_ref, k_ref, v_ref, g_ref, sb_ref, gng_ref, gnb_ref, o_ref,
                state_ref, dmat_ref, kdec_ref, qdec_ref, qdecb_ref, cdec_ref, *, cr):
    @pl.when(pl.program_id(1) == 0)
    def _():
        state_ref[...] = jnp.zeros_like(state_ref)
        ri = lax.broadcasted_iota(jnp.int32, (cr, cr), 0)
        ci = lax.broadcasted_iota(jnp.int32, (cr, cr), 1)
        relf = (ri - ci).astype(F32)
        idx = lax.broadcasted_iota(jnp.int32, (cr, HB_DIM), 0).astype(F32)
        for h in range(H_B):
            low = jnp.exp(jnp.maximum(relf, 0.0) * _log_decay(lf_ref[h], (cr, cr)))
            up = jnp.exp(jnp.maximum(-relf, 0.0) * _log_decay(lb_ref[h], (cr, cr)))
            dmat_ref[h] = jnp.where(ri >= ci, low, up).astype(BF16)
            lgv = _log_decay(lf_ref[h], (cr, HB_DIM))
            kdec_ref[h] = jnp.exp((cr - 1.0 - idx) * lgv)
            qdec_ref[h] = jnp.exp((idx + 1.0) * lgv).astype(BF16)
            qdecb_ref[h] = jnp.exp((cr - idx) * _log_decay(lb_ref[h], (cr, HB_DIM))).astype(BF16)
            cdec_ref[h] = jnp.exp(float(cr) * _log_decay(lf_ref[h], (HB_DIM, HB_DIM)))

    for h in range(H_B):
        cols = slice(h * HB_DIM, (h + 1) * HB_DIM)
        q = q_ref[:, cols]
        k = k_ref[:, cols]
        v = v_ref[:, cols]
        scores = lax.dot_general(q, k, (((1,), (1,)), ((), ())), preferred_element_type=F32)
        o = jnp.dot(scores.astype(BF16) * dmat_ref[h], v, preferred_element_type=F32)
        state = state_ref[h]
        o = o + jnp.dot(q * qdec_ref[h], state.astype(BF16), preferred_element_type=F32)
        o = o + jnp.dot(q * qdecb_ref[h], sb_ref[0, h], preferred_element_type=F32)
        kd_t = (k.astype(F32) * kdec_ref[h]).T.astype(BF16)
        state_ref[h] = cdec_ref[h] * state + jnp.dot(kd_t, v, preferred_element_type=F32)
        mu = jnp.mean(o, axis=-1, keepdims=True)
        dlt = o - mu
        var = jnp.mean(dlt * dlt, axis=-1, keepdims=True)
        on = dlt * lax.rsqrt(var + LN_EPS) * gng_ref[:, cols] + gnb_ref[:, cols]
        o_ref[:, cols] = (g_ref[:, cols].astype(F32) * on).astype(BF16)


def _retention(logit_f, logit_b, q, k, v, g, state_b, gn_g, gn_b, seq_len):
    n, _ = q.shape
    cr = min(RET_CHUNK, seq_len)
    nc = seq_len // cr
    batch = n // seq_len
    smem = pl.BlockSpec(memory_space=pltpu.SMEM)
    state = pltpu.VMEM((H_B, HB_DIM, HB_DIM), F32)
    qdec = pltpu.VMEM((H_B, cr, HB_DIM), BF16)
    blk = pl.BlockSpec((cr, D_B), lambda b, c: (b * nc + c, 0))
    return pl.pallas_call(
        functools.partial(_ret_kernel, cr=cr),
        grid=(batch, nc),
        in_specs=[smem, smem, blk, blk, blk, blk,
                  pl.BlockSpec((1, H_B, HB_DIM, HB_DIM), lambda b, c: (b * nc + c, 0, 0, 0)),
                  _resident((1, D_B)), _resident((1, D_B))],
        out_specs=blk,
        out_shape=jax.ShapeDtypeStruct((n, D_B), BF16),
        scratch_shapes=[state, pltpu.VMEM((H_B, cr, cr), BF16), pltpu.VMEM((H_B, cr, HB_DIM), F32),
                        qdec, qdec, state],
        compiler_params=_params(2),
        name="retention",
    )(logit_f, logit_b, q, k, v, g, state_b, gn_g, gn_b)


def _fft1_kernel(f_ref, z_ref, t_ref, zs_ref, ts_ref, *, nb, l1):
    for gi in range(G_C):
        zs_ref[gi] = z_ref[gi].reshape(l1 * nb, GC_DIM)
    for j in range(nb):
        zj = jnp.concatenate([zs_ref[gi, pl.ds(j, l1, stride=nb), :] for gi in range(G_C)], axis=1)
        t = jnp.dot(f_ref[...], zj.astype(BF16), preferred_element_type=F32)
        for gi in range(G_C):
            ts_ref[gi, pl.ds(j, 2 * l1, stride=nb), :] = t[:, gi * GC_DIM:(gi + 1) * GC_DIM]
    for gi in range(G_C):
        t_ref[gi] = ts_ref[gi].reshape(2 * l1, nb, GC_DIM)


def _fft3_kernel(g_ref, t_ref, cs_ref, o_ref, os_ref, *, k1b, l2):
    n2_blocks = t_ref.shape[1]
    for j in range(k1b):
        t = jnp.concatenate(
            [jnp.concatenate(
                [jnp.concatenate([t_ref[gi, c, ri, j] for c in range(n2_blocks)], axis=0)
                 for gi in range(G_C)], axis=1) for ri in range(2)],
            axis=0).astype(BF16)
        p = jnp.dot(g_ref[j], t, preferred_element_type=F32)
        for gi in range(G_C):
            cols = slice(gi * GC_DIM, (gi + 1) * GC_DIM)
            pc = jnp.concatenate([p[:l2, cols], p[l2:, cols]], axis=1).astype(BF16)
            os_ref[gi, pl.ds(j, l2, stride=k1b), :] = jnp.dot(pc, cs_ref[...], preferred_element_type=F32)
    for gi in range(G_C):
        o_ref[gi] = os_ref[gi].reshape(l2, k1b, GC_DIM)


def _fourier_tables(seq_len):
    l2 = min(FFT_L2, seq_len)
    l1 = seq_len // l2
    two_pi = 2.0 * math.pi
    i1 = jnp.arange(l1, dtype=jnp.int32)
    a1 = ((i1[:, None] * i1[None, :]) % l1).astype(F32) * (two_pi / l1)
    f1 = jnp.concatenate([jnp.cos(a1), -jnp.sin(a1)], axis=0) * (l1 ** -0.5)
    i2 = jnp.arange(l2, dtype=jnp.int32)
    kk = i1[:, None, None] + l1 * i2[None, :, None]
    a2 = ((kk * i2[None, None, :]) % seq_len).astype(F32) * (two_pi / seq_len)
    gr = jnp.cos(a2) * (l2 ** -0.5)
    gi = -jnp.sin(a2) * (l2 ** -0.5)
    g = jnp.concatenate([jnp.concatenate([gr, -gi], axis=2),
                         jnp.concatenate([gi, gr], axis=2)], axis=1)
    ic = jnp.arange(GC_DIM, dtype=jnp.int32)
    ac = ((ic[:, None] * ic[None, :]) % GC_DIM).astype(F32) * (two_pi / GC_DIM)
    cs = jnp.concatenate([jnp.cos(ac), jnp.sin(ac)], axis=0) * (GC_DIM ** -0.5)
    return f1.astype(BF16), g.astype(BF16), cs.astype(BF16)


def _fourier(zc, tables):
    f1, g, cs = tables
    _, batch, n2_blocks, l1, nb, _ = zc.shape
    l2 = n2_blocks * nb
    t = pl.pallas_call(
        functools.partial(_fft1_kernel, nb=nb, l1=l1),
        grid=(batch, n2_blocks),
        in_specs=[_resident((2 * l1, l1)),
                  pl.BlockSpec((G_C, None, None, l1, nb, GC_DIM), lambda b, c: (0, b, c, 0, 0, 0))],
        out_specs=pl.BlockSpec((G_C, None, None, 2 * l1, nb, GC_DIM), lambda b, c: (0, b, c, 0, 0, 0)),
        out_shape=jax.ShapeDtypeStruct((G_C, batch, n2_blocks, 2 * l1, nb, GC_DIM), F32),
        scratch_shapes=[pltpu.VMEM((G_C, l1 * nb, GC_DIM), F32),
                        pltpu.VMEM((G_C, 2 * l1 * nb, GC_DIM), F32)],
        compiler_params=_params(2),
        name="fourier_stage1",
    )(f1, zc)
    k1b = min(FFT_K1B, l1)
    return pl.pallas_call(
        functools.partial(_fft3_kernel, k1b=k1b, l2=l2),
        grid=(batch, l1 // k1b),
        in_specs=[pl.BlockSpec((k1b, 2 * l2, 2 * l2), lambda b, c: (c, 0, 0)),
                  pl.BlockSpec((G_C, None, n2_blocks, 2, k1b, nb, GC_DIM),
                               lambda b, c: (0, b, 0, 0, c, 0, 0)),
                  _resident((2 * GC_DIM, GC_DIM))],
        out_specs=pl.BlockSpec((G_C, None, None, l2, k1b, GC_DIM), lambda b, c: (0, b, c, 0, 0, 0)),
        out_shape=jax.ShapeDtypeStruct((G_C, batch, l1 // k1b, l2, k1b, GC_DIM), F32),
        scratch_shapes=[pltpu.VMEM((G_C, l2 * k1b, GC_DIM), F32)],
        compiler_params=_params(2),
        name="fourier_stage2",
    )(g, t.reshape(G_C, batch, n2_blocks, 2, l1, nb, GC_DIM), cs)


def _combine_kernel(x_ref, gate_ref, ya_ref, yb_ref, yc_ref, wa_ref, wb_ref, wc_ref, wo_ref,
                    g_ref, b_ref, o_ref):
    _, k1_blocks, k2_count, _, _ = yc_ref.shape
    yc = jnp.concatenate(
        [jnp.concatenate([yc_ref[gi, kb, k2] for k2 in range(k2_count) for kb in range(k1_blocks)], axis=0)
         for gi in range(G_C)], axis=1).astype(BF16)
    s = None
    for j, (br, w_ref) in enumerate(((ya_ref[...], wa_ref), (yb_ref[...], wb_ref), (yc, wc_ref))):
        y = jnp.dot(br, w_ref[...], preferred_element_type=F32)
        term = gate_ref[:, j * D_MODEL:(j + 1) * D_MODEL].astype(F32) * y
        s = term if s is None else s + term
    mix = jnp.dot(s.astype(BF16), wo_ref[...], preferred_element_type=F32)
    o_ref[...] = _layer_norm(ALPHA * x_ref[...] + mix, g_ref[...], b_ref[...])


def _combine(x, gates, ya, yb, yc, wa, wb, wc, wo, ln_g, ln_b):
    n, d = x.shape
    _, batch, k1_blocks, l2, k1b, _ = yc.shape
    seq_len = n // batch
    l1 = k1_blocks * k1b
    tm = min(TM_FFN, seq_len)
    tps = seq_len // tm
    row = lambda w: pl.BlockSpec((tm, w), lambda b, t: (b * tps + t, 0))
    return pl.pallas_call(
        _combine_kernel,
        grid=(batch, tps),
        in_specs=[row(d), row(N_BRANCH * d), row(D_A), row(D_B),
                  pl.BlockSpec((G_C, None, k1_blocks, tm // l1, k1b, GC_DIM),
                               lambda b, t: (0, b, 0, t, 0, 0)),
                  _resident((D_A, d)), _resident((D_B, d)), _resident((D_C, d)), _resident((d, d)),
                  _resident((1, d)), _resident((1, d))],
        out_specs=row(d),
        out_shape=jax.ShapeDtypeStruct((n, d), F32),
        compiler_params=_params(2),
        name="combine_ln",
    )(x, gates, ya, yb, yc, wa, wb, wc, wo, ln_g, ln_b)


def _rotary_tables(seq_len):
    half = HB_DIM // 2
    inv = ROPE_BASE ** (-jnp.arange(half, dtype=F32) / half)
    ang = jnp.arange(seq_len, dtype=F32)[:, None] * inv[None, :]
    cos = jnp.cos(ang)
    sin = jnp.sin(ang)
    return jnp.concatenate([cos, cos], axis=1), jnp.concatenate([-sin, sin], axis=1)


def _prep_layer(p, l):
    mat = lambda w: w[l].astype(BF16)
    vec = lambda w: w[l].reshape(1, -1).astype(F32)
    return dict(
        ffn1=(mat(p["ffn1_w_gate"]), mat(p["ffn1_w_up"]), mat(p["ffn1_w_down"]),
              vec(p["ln1_g"]), vec(p["ln1_b"])),
        ffn2=(mat(p["ffn2_w_gate"]), mat(p["ffn2_w_up"]), mat(p["ffn2_w_down"]),
              vec(p["ln3_g"]), vec(p["ln3_b"])),
        inproj=(p["w_in"][l].astype(BF16), vec(p["b_in"]), vec(p["sgu_ln_g"]), vec(p["sgu_ln_b"]),
                p["sgu_w"][l].astype(BF16),
                jnp.broadcast_to(p["sgu_b"][l][:, :, None], (G_A, CHUNK, GA_DIM)).astype(F32)),
        logit_f=p["ret_logit_fwd"][l].astype(F32),
        logit_b=p["ret_logit_bwd"][l].astype(F32),
        gn=(vec(p["ret_gn_g"]), vec(p["ret_gn_b"])),
        combine=(p["w_br_a"][l].astype(BF16), p["w_br_b"][l].astype(BF16), p["w_br_c"][l].astype(BF16),
                 p["w_o"][l].astype(BF16), vec(p["ln2_g"]), vec(p["ln2_b"])),
    )


def _trunk(x, layers):
    batch, seq_len, d = x.shape
    x = x.reshape(batch * seq_len, d)
    cos2, sin2 = _rotary_tables(seq_len)
    tables = _fourier_tables(seq_len)
    for lw in layers:
        x = _ffn(x, *lw["ffn1"])
        gates, ya, q, k, v, g, zc, state_b = _inproj(x, seq_len, lw["logit_b"], *lw["inproj"], cos2, sin2)
        yb = _retention(lw["logit_f"], lw["logit_b"], q, k, v, g, state_b, *lw["gn"], seq_len)
        yc = _fourier(zc, tables)
        x = _combine(x, gates, ya, yb, yc, *lw["combine"])
        x = _ffn(x, *lw["ffn2"])
    return x.reshape(batch, seq_len, d)


def kernel(x_prompt, x_sample, ffn1_w_gate, ffn1_w_up, ffn1_w_down, ln1_g, ln1_b, w_in, b_in, sgu_ln_g, sgu_ln_b, sgu_w, sgu_b, ret_logit_fwd, ret_logit_bwd, ret_gn_g, ret_gn_b, w_br_a, w_br_b, w_br_c, w_o, ln2_g, ln2_b, ffn2_w_gate, ffn2_w_up, ffn2_w_down, ln3_g, ln3_b):
    p = dict(ffn1_w_gate=ffn1_w_gate, ffn1_w_up=ffn1_w_up, ffn1_w_down=ffn1_w_down, ln1_g=ln1_g,
             ln1_b=ln1_b, w_in=w_in, b_in=b_in, sgu_ln_g=sgu_ln_g, sgu_ln_b=sgu_ln_b, sgu_w=sgu_w,
             sgu_b=sgu_b, ret_logit_fwd=ret_logit_fwd, ret_logit_bwd=ret_logit_bwd, ret_gn_g=ret_gn_g,
             ret_gn_b=ret_gn_b, w_br_a=w_br_a, w_br_b=w_br_b, w_br_c=w_br_c, w_o=w_o, ln2_g=ln2_g,
             ln2_b=ln2_b, ffn2_w_gate=ffn2_w_gate, ffn2_w_up=ffn2_w_up, ffn2_w_down=ffn2_w_down,
             ln3_g=ln3_g, ln3_b=ln3_b)
    layers = [_prep_layer(p, l) for l in range(DEPTH)]
    return (_trunk(x_prompt, layers), _trunk(x_sample, layers))
```

```python
import functools
import math

import jax
import jax.numpy as jnp
from jax import lax
from jax.experimental import pallas as pl
from jax.experimental.pallas import tpu as pltpu

F32 = jnp.float32
BF16 = jnp.bfloat16

D_MODEL = 1024
DEPTH = 2
CHUNK = 128
D_A = 512
G_A = 4
GA_DIM = D_A // G_A
D_B = 512
H_B = 4
HB_DIM = D_B // H_B
D_C = 512
G_C = 4
GC_DIM = D_C // G_C
N_BRANCH = 3
D_FF = 2816
ROPE_BASE = 10000.0
LN_EPS = 1e-5
ALPHA = (2.0 * DEPTH) ** 0.25
D_IN = N_BRANCH * D_MODEL + 2 * D_A + 4 * D_B + D_C

OFF_GATE = 0
OFF_ZA = N_BRANCH * D_MODEL
OFF_Q = OFF_ZA + 2 * D_A
OFF_K = OFF_Q + D_B
OFF_V = OFF_K + D_B
OFF_G = OFF_V + D_B
OFF_ZC = OFF_G + D_B

LANES = 128
FF_CHUNK = 256
TM_FFN = 1024
RET_CHUNK = 512
FFT_L2 = 128
FFT_N2B = 8
FFT_K1B = 8
VMEM_LIMIT = 56 * 1024 * 1024


def _params(n_axes):
    return pltpu.CompilerParams(dimension_semantics=("arbitrary",) * n_axes,
                                vmem_limit_bytes=VMEM_LIMIT)


def _resident(shape):
    zeros = (0,) * len(shape)
    return pl.BlockSpec(shape, lambda *_: zeros, pipeline_mode=pl.Buffered(1))


def _layer_norm(y, g, b):
    mu = jnp.mean(y, axis=-1, keepdims=True)
    d = y - mu
    var = jnp.mean(d * d, axis=-1, keepdims=True)
    return d * lax.rsqrt(var + LN_EPS) * g + b


def _silu(x):
    return x * jax.nn.sigmoid(x)


def _gelu_tanh(x):
    c = math.sqrt(2.0 / math.pi)
    return 0.5 * x * (1.0 + jnp.tanh(c * (x + 0.044715 * (x * x * x))))


def _log_sigmoid(x):
    return jnp.minimum(x, 0.0) - jnp.log1p(jnp.exp(-jnp.abs(x)))


def _ffn_kernel(x_ref, wg_ref, wu_ref, wd_ref, g_ref, b_ref, o_ref, xb_ref, acc_ref):
    xb_ref[...] = x_ref[...].astype(BF16)
    n_chunks = D_FF // FF_CHUNK
    for c in range(n_chunks):
        cols = slice(c * FF_CHUNK, (c + 1) * FF_CHUNK)
        xb = xb_ref[...]
        gate = jnp.dot(xb, wg_ref[:, cols], preferred_element_type=F32)
        up = jnp.dot(xb, wu_ref[:, cols], preferred_element_type=F32)
        h = (_silu(gate) * up).astype(BF16)
        down = jnp.dot(h, wd_ref[cols, :], preferred_element_type=F32)
        if c == 0:
            acc_ref[...] = down
        elif c < n_chunks - 1:
            acc_ref[...] += down
        else:
            y = ALPHA * x_ref[...] + 0.5 * (acc_ref[...] + down)
            o_ref[...] = _layer_norm(y, g_ref[...], b_ref[...])


def _ffn(x, wg, wu, wd, ln_g, ln_b):
    n, d = x.shape
    tm = min(TM_FFN, n)
    return pl.pallas_call(
        _ffn_kernel,
        grid=(n // tm,),
        in_specs=[
            pl.BlockSpec((tm, d), lambda i: (i, 0)),
            _resident((d, D_FF)),
            _resident((d, D_FF)),
            _resident((D_FF, d)),
            _resident((1, d)),
            _resident((1, d)),
        ],
        out_specs=pl.BlockSpec((tm, d), lambda i: (i, 0)),
        out_shape=jax.ShapeDtypeStruct((n, d), F32),
        scratch_shapes=[pltpu.VMEM((tm, d), BF16), pltpu.VMEM((tm, d), F32)],
        compiler_params=_params(1),
        name="ffn_ln",
    )(x, wg, wu, wd, ln_g, ln_b)


def _inproj_kernel(lb_ref, x_ref, w_ref, b_ref, lng_ref, lnb_ref, ws_ref, bs_ref, cos_ref, sin_ref,
                   gate_ref, ya_ref, q_ref, k_ref, v_ref, g_ref, zc_ref, sb_ref,
                   xb_ref, state_ref, kdec_ref, cdec_ref, *, tm):
    @pl.when(pl.program_id(1) == 0)
    def _():
        state_ref[...] = jnp.zeros_like(state_ref)
        idx = lax.broadcasted_iota(jnp.int32, (tm, HB_DIM), 0).astype(F32)
        for h in range(H_B):
            kdec_ref[h] = jnp.exp(idx * _log_decay(lb_ref[h], (tm, HB_DIM)))
            cdec_ref[h] = jnp.exp(float(tm) * _log_decay(lb_ref[h], (HB_DIM, HB_DIM)))

    xb_ref[...] = x_ref[...].astype(BF16)

    def proj(lo, width):
        return (jnp.dot(xb_ref[...], w_ref[:, lo:lo + width], preferred_element_type=F32)
                + b_ref[:, lo:lo + width])

    for j in range(N_BRANCH):
        gate_ref[:, j * D_MODEL:(j + 1) * D_MODEL] = jax.nn.sigmoid(
            proj(OFF_GATE + j * D_MODEL, D_MODEL)).astype(BF16)

    za = _gelu_tanh(proj(OFF_ZA, 2 * D_A))
    u = za[:, :D_A]
    vn = _layer_norm(za[:, D_A:], lng_ref[...], lnb_ref[...]).astype(BF16)
    for r in range(tm // CHUNK):
        rows = slice(r * CHUNK, (r + 1) * CHUNK)
        for gi in range(G_A):
            cols = slice(gi * GA_DIM, (gi + 1) * GA_DIM)
            mixed = jnp.dot(ws_ref[gi], vn[rows, cols], preferred_element_type=F32) + bs_ref[gi]
            ya_ref[rows, cols] = (u[rows, cols] * mixed).astype(BF16)

    cos = cos_ref[...]
    sin = sin_ref[...]
    qk = proj(OFF_Q, 2 * D_B)
    vg = proj(OFF_V, 2 * D_B)
    v = vg[:, :D_B].astype(BF16)
    v_ref[...] = v
    g_ref[...] = _silu(vg[:, D_B:]).astype(BF16)
    for hd in range(H_B):
        cols = slice(hd * HB_DIM, (hd + 1) * HB_DIM)
        s = qk[:, hd * HB_DIM:(hd + 1) * HB_DIM]
        q_ref[:, cols] = ((s * cos + pltpu.roll(s, HB_DIM // 2, 1) * sin)
                          * (HB_DIM ** -0.5)).astype(BF16)
        s = qk[:, D_B + hd * HB_DIM:D_B + (hd + 1) * HB_DIM]
        k = (s * cos + pltpu.roll(s, HB_DIM // 2, 1) * sin).astype(BF16)
        k_ref[:, cols] = k
        state = state_ref[hd]
        sb_ref[0, hd] = state.astype(BF16)
        kd_t = (k.astype(F32) * kdec_ref[hd]).T.astype(BF16)
        state_ref[hd] = cdec_ref[hd] * state + jnp.dot(kd_t, v[:, cols], preferred_element_type=F32)

    zc = proj(OFF_ZC, D_C)
    n2_blocks = CHUNK // FFT_N2B
    for gi in range(G_C):
        for r in range(tm // CHUNK):
            zc_ref[gi, :, r] = zc[r * CHUNK:(r + 1) * CHUNK, gi * GC_DIM:(gi + 1) * GC_DIM].reshape(
                n2_blocks, FFT_N2B, GC_DIM)

def _inproj(x, seq_len, logit_b, w_in, b_in, sgu_ln_g, sgu_ln_b, sgu_w, sgu_bb, cos2, sin2):
    n, d = x.shape
    tm = min(RET_CHUNK, seq_len)
    tps = seq_len // tm
    batch = n // seq_len
    l1 = seq_len // CHUNK
    n2_blocks = CHUNK // FFT_N2B
    row = lambda w: pl.BlockSpec((tm, w), lambda b, j: (b * tps + tps - 1 - j, 0))
    pos = pl.BlockSpec((tm, HB_DIM), lambda b, j: (tps - 1 - j, 0))
    out = lambda w: jax.ShapeDtypeStruct((n, w), BF16)
    return pl.pallas_call(
        functools.partial(_inproj_kernel, tm=tm),
        grid=(batch, tps),
        in_specs=[
            pl.BlockSpec(memory_space=pltpu.SMEM),
            row(d),
            _resident((d, D_IN)),
            _resident((1, D_IN)),
            _resident((1, D_A)),
            _resident((1, D_A)),
            _resident((G_A, CHUNK, CHUNK)),
            _resident((G_A, CHUNK, GA_DIM)),
            pos,
            pos,
        ],
        out_specs=[row(N_BRANCH * D_MODEL), row(D_A), row(D_B), row(D_B), row(D_B), row(D_B),
                   pl.BlockSpec((G_C, None, n2_blocks, tm // CHUNK, FFT_N2B, GC_DIM),
                                lambda b, j: (0, b, 0, tps - 1 - j, 0, 0)),
                   pl.BlockSpec((1, H_B, HB_DIM, HB_DIM), lambda b, j: (b * tps + tps - 1 - j, 0, 0, 0))],
        out_shape=[out(N_BRANCH * D_MODEL), out(D_A), out(D_B), out(D_B), out(D_B), out(D_B),
                   jax.ShapeDtypeStruct((G_C, batch, n2_blocks, l1, FFT_N2B, GC_DIM), F32),
                   jax.ShapeDtypeStruct((n // tm, H_B, HB_DIM, HB_DIM), BF16)],
        scratch_shapes=[pltpu.VMEM((tm, d), BF16),
                        pltpu.VMEM((H_B, HB_DIM, HB_DIM), F32),
                        pltpu.VMEM((H_B, tm, HB_DIM), F32),
                        pltpu.VMEM((H_B, HB_DIM, HB_DIM), F32)],
        compiler_params=_params(2),
        name="inproj",
    )(logit_b, x, w_in, b_in, sgu_ln_g, sgu_ln_b, sgu_w, sgu_bb, cos2, sin2)


def _log_decay(logit, shape):
    return _log_sigmoid(jnp.full(shape, logit, F32))


def _ret_kernel(lf_ref, lb_ref, q_ref, k_ref, v_ref, g_ref, sb_ref, gng_ref, gnb_ref, o_ref,
                state_ref, dmat_ref, kdec_ref, qdec_ref, qdecb_ref, cdec_ref, *, cr):
    @pl.when(pl.program_id(1) == 0)
    def _():
        state_ref[...] = jnp.zeros_like(state_ref)
        ri = lax.broadcasted_iota(jnp.int32, (cr, cr), 0)
        ci = lax.broadcasted_iota(jnp.int32, (cr, cr), 1)
        relf = (ri - ci).astype(F32)
        idx = lax.broadcasted_iota(jnp.int32, (cr, HB_DIM), 0).astype(F32)
        for h in range(H_B):
            low = jnp.exp(jnp.maximum(relf, 0.0) * _log_decay(lf_ref[h], (cr, cr)))
            up = jnp.exp(jnp.maximum(-relf, 0.0) * _log_decay(lb_ref[h], (cr, cr)))
            dmat_ref[h] = jnp.where(ri >= ci, low, up).astype(BF16)
            lgv = _log_decay(lf_ref[h], (cr, HB_DIM))
            kdec_ref[h] = jnp.exp((cr - 1.0 - idx) * lgv)
            qdec_ref[h] = jnp.exp((idx + 1.0) * lgv).astype(BF16)
            qdecb_ref[h] = jnp.exp((cr - idx) * _log_decay(lb_ref[h], (cr, HB_DIM))).astype(BF16)
            cdec_ref[h] = jnp.exp(float(cr) * _log_decay(lf_ref[h], (HB_DIM, HB_DIM)))

    for h in range(H_B):
        cols = slice(h * HB_DIM, (h + 1) * HB_DIM)
        q = q_ref[:, cols]
        k = k_ref[:, cols]
        v = v_ref[:, cols]
        scores = lax.dot_general(q, k, (((1,), (1,)), ((), ())), preferred_element_type=F32)
        o = jnp.dot(scores.astype(BF16) * dmat_ref[h], v, preferred_element_type=F32)
        state = state_ref[h]
        o = o + jnp.dot(q * qdec_ref[h], state.astype(BF16), preferred_element_type=F32)
        o = o + jnp.dot(q * qdecb_ref[h], sb_ref[0, h], preferred_element_type=F32)
        kd_t = (k.astype(F32) * kdec_ref[h]).T.astype(BF16)
        state_ref[h] = cdec_ref[h] * state + jnp.dot(kd_t, v, preferred_element_type=F32)
        mu = jnp.mean(o, axis=-1, keepdims=True)
        dlt = o - mu
        var = jnp.mean(dlt * dlt, axis=-1, keepdims=True)
        on = dlt * lax.rsqrt(var + LN_EPS) * gng_ref[:, cols] + gnb_ref[:, cols]
        o_ref[:, cols] = (g_ref[:, cols].astype(F32) * on).astype(BF16)


def _retention(logit_f, logit_b, q, k, v, g, state_b, gn_g, gn_b, seq_len):
    n, _ = q.shape
    cr = min(RET_CHUNK, seq_len)
    nc = seq_len // cr
    batch = n // seq_len
    smem = pl.BlockSpec(memory_space=pltpu.SMEM)
    state = pltpu.VMEM((H_B, HB_DIM, HB_DIM), F32)
    qdec = pltpu.VMEM((H_B, cr, HB_DIM), BF16)
    blk = pl.BlockSpec((cr, D_B), lambda b, c: (b * nc + c, 0))
    return pl.pallas_call(
        functools.partial(_ret_kernel, cr=cr),
        grid=(batch, nc),
        in_specs=[smem, smem, blk, blk, blk, blk,
                  pl.BlockSpec((1, H_B, HB_DIM, HB_DIM), lambda b, c: (b * nc + c, 0, 0, 0)),
                  _resident((1, D_B)), _resident((1, D_B))],
        out_specs=blk,
        out_shape=jax.ShapeDtypeStruct((n, D_B), BF16),
        scratch_shapes=[state, pltpu.VMEM((H_B, cr, cr), BF16), pltpu.VMEM((H_B, cr, HB_DIM), F32),
                        qdec, qdec, state],
        compiler_params=_params(2),
        name="retention",
    )(logit_f, logit_b, q, k, v, g, state_b, gn_g, gn_b)


def _pack_bf16_pair(hi, lo):
    hi_bits = lax.bitcast_convert_type(hi.astype(BF16).astype(F32), jnp.uint32)
    lo_bits = lax.bitcast_convert_type(lo.astype(BF16).astype(F32), jnp.uint32)
    return hi_bits | (lo_bits >> 16)


def _unpack_bf16_pair(w):
    hi = lax.bitcast_convert_type(w & jnp.uint32(0xFFFF0000), F32)
    lo = lax.bitcast_convert_type(w << 16, F32)
    return hi.astype(BF16), lo.astype(BF16)


def _fft1_kernel(f_ref, z_ref, t_ref, zs_ref, ts_ref, *, nb, l1):
    for gi in range(G_C):
        zs_ref[gi] = z_ref[gi].reshape(l1 * nb, GC_DIM)
    for j in range(nb):
        zj = jnp.concatenate([zs_ref[gi, pl.ds(j, l1, stride=nb), :] for gi in range(G_C)], axis=1)
        t = jnp.dot(f_ref[...], zj.astype(BF16), preferred_element_type=F32)
        w = _pack_bf16_pair(t[:l1], t[l1:])
        for gi in range(G_C):
            ts_ref[gi, pl.ds(j, l1, stride=nb), :] = w[:, gi * GC_DIM:(gi + 1) * GC_DIM]
    for gi in range(G_C):
        t_ref[gi] = ts_ref[gi].reshape(l1, nb, GC_DIM)


def _fft3_kernel(g_ref, t_ref, cs_ref, o_ref, os_ref, *, k1b, l2):
    n2_blocks = t_ref.shape[1]
    for j in range(k1b):
        w = jnp.concatenate(
            [jnp.concatenate([t_ref[gi, c, j] for c in range(n2_blocks)], axis=0) for gi in range(G_C)],
            axis=1)
        t = jnp.concatenate(_unpack_bf16_pair(w), axis=0)
        p = jnp.dot(g_ref[j], t, preferred_element_type=F32)
        for gi in range(G_C):
            cols = slice(gi * GC_DIM, (gi + 1) * GC_DIM)
            pc = jnp.concatenate([p[:l2, cols], p[l2:, cols]], axis=1).astype(BF16)
            os_ref[gi, pl.ds(j, l2, stride=k1b), :] = jnp.dot(pc, cs_ref[...], preferred_element_type=F32)
    for gi in range(G_C):
        o_ref[gi] = os_ref[gi].reshape(l2, k1b, GC_DIM)


def _fourier_tables(seq_len):
    l2 = min(FFT_L2, seq_len)
    l1 = seq_len // l2
    two_pi = 2.0 * math.pi
    i1 = jnp.arange(l1, dtype=jnp.int32)
    a1 = ((i1[:, None] * i1[None, :]) % l1).astype(F32) * (two_pi / l1)
    f1 = jnp.concatenate([jnp.cos(a1), -jnp.sin(a1)], axis=0) * (l1 ** -0.5)
    i2 = jnp.arange(l2, dtype=jnp.int32)
    aa = ((i1[:, None] * i2[None, :]) % seq_len).astype(F32) * (two_pi / seq_len)
    ab = ((i2[:, None] * i2[None, :]) % l2).astype(F32) * (two_pi / l2)
    ca, sa = jnp.cos(aa)[:, None, :], jnp.sin(aa)[:, None, :]
    cb, sb = jnp.cos(ab)[None, :, :], jnp.sin(ab)[None, :, :]
    gr = (ca * cb - sa * sb) * (l2 ** -0.5)
    gi = -(sa * cb + ca * sb) * (l2 ** -0.5)
    g = jnp.concatenate([jnp.concatenate([gr, -gi], axis=2),
                         jnp.concatenate([gi, gr], axis=2)], axis=1)
    ic = jnp.arange(GC_DIM, dtype=jnp.int32)
    ac = ((ic[:, None] * ic[None, :]) % GC_DIM).astype(F32) * (two_pi / GC_DIM)
    cs = jnp.concatenate([jnp.cos(ac), jnp.sin(ac)], axis=0) * (GC_DIM ** -0.5)
    return f1.astype(BF16), g.astype(BF16), cs.astype(BF16)


def _fourier(zc, tables):
    f1, g, cs = tables
    _, batch, n2_blocks, l1, nb, _ = zc.shape
    l2 = n2_blocks * nb
    t = pl.pallas_call(
        functools.partial(_fft1_kernel, nb=nb, l1=l1),
        grid=(batch, n2_blocks),
        in_specs=[_resident((2 * l1, l1)),
                  pl.BlockSpec((G_C, None, None, l1, nb, GC_DIM), lambda b, c: (0, b, c, 0, 0, 0))],
        out_specs=pl.BlockSpec((G_C, None, None, l1, nb, GC_DIM), lambda b, c: (0, b, c, 0, 0, 0)),
        out_shape=jax.ShapeDtypeStruct((G_C, batch, n2_blocks, l1, nb, GC_DIM), jnp.uint32),
        scratch_shapes=[pltpu.VMEM((G_C, l1 * nb, GC_DIM), F32),
                        pltpu.VMEM((G_C, l1 * nb, GC_DIM), jnp.uint32)],
        compiler_params=_params(2),
        name="fourier_stage1",
    )(f1, zc)
    k1b = min(FFT_K1B, l1)
    return pl.pallas_call(
        functools.partial(_fft3_kernel, k1b=k1b, l2=l2),
        grid=(batch, l1 // k1b),
        in_specs=[pl.BlockSpec((k1b, 2 * l2, 2 * l2), lambda b, c: (c, 0, 0)),
                  pl.BlockSpec((G_C, None, n2_blocks, k1b, nb, GC_DIM),
                               lambda b, c: (0, b, 0, c, 0, 0)),
                  _resident((2 * GC_DIM, GC_DIM))],
        out_specs=pl.BlockSpec((G_C, None, None, l2, k1b, GC_DIM), lambda b, c: (0, b, c, 0, 0, 0)),
        out_shape=jax.ShapeDtypeStruct((G_C, batch, l1 // k1b, l2, k1b, GC_DIM), F32),
        scratch_shapes=[pltpu.VMEM((G_C, l2 * k1b, GC_DIM), F32)],
        compiler_params=_params(2),
        name="fourier_stage2",
    )(g, t, cs)


def _combine_kernel(x_ref, gate_ref, ya_ref, yb_ref, yc_ref, wa_ref, wb_ref, wc_ref, wo_ref,
                    g_ref, b_ref, o_ref):
    _, k1_blocks, k2_count, _, _ = yc_ref.shape
    yc = jnp.concatenate(
        [jnp.concatenate([yc_ref[gi, kb, k2] for k2 in range(k2_count) for kb in range(k1_blocks)], axis=0)
         for gi in range(G_C)], axis=1).astype(BF16)
    s = None
    for j, (br, w_ref) in enumerate(((ya_ref[...], wa_ref), (yb_ref[...], wb_ref), (yc, wc_ref))):
        y = jnp.dot(br, w_ref[...], preferred_element_type=F32)
        term = gate_ref[:, j * D_MODEL:(j + 1) * D_MODEL].astype(F32) * y
        s = term if s is None else s + term
    mix = jnp.dot(s.astype(BF16), wo_ref[...], preferred_element_type=F32)
    o_ref[...] = _layer_norm(ALPHA * x_ref[...] + mix, g_ref[...], b_ref[...])


def _combine(x, gates, ya, yb, yc, wa, wb, wc, wo, ln_g, ln_b):
    n, d = x.shape
    _, batch, k1_blocks, l2, k1b, _ = yc.shape
    seq_len = n // batch
    l1 = k1_blocks * k1b
    tm = min(TM_FFN, seq_len)
    tps = seq_len // tm
    row = lambda w: pl.BlockSpec((tm, w), lambda b, t: (b * tps + t, 0))
    return pl.pallas_call(
        _combine_kernel,
        grid=(batch, tps),
        in_specs=[row(d), row(N_BRANCH * d), row(D_A), row(D_B),
                  pl.BlockSpec((G_C, None, k1_blocks, tm // l1, k1b, GC_DIM),
                               lambda b, t: (0, b, 0, t, 0, 0)),
                  _resident((D_A, d)), _resident((D_B, d)), _resident((D_C, d)), _resident((d, d)),
                  _resident((1, d)), _resident((1, d))],
        out_specs=row(d),
        out_shape=jax.ShapeDtypeStruct((n, d), F32),
        compiler_params=_params(2),
        name="combine_ln",
    )(x, gates, ya, yb, yc, wa, wb, wc, wo, ln_g, ln_b)


def _rotary_tables(seq_len):
    half = HB_DIM // 2
    inv = ROPE_BASE ** (-jnp.arange(half, dtype=F32) / half)
    ang = jnp.arange(seq_len, dtype=F32)[:, None] * inv[None, :]
    cos = jnp.cos(ang)
    sin = jnp.sin(ang)
    return jnp.concatenate([cos, cos], axis=1), jnp.concatenate([-sin, sin], axis=1)


def _prep_layer(p, l):
    mat = lambda w: w[l].astype(BF16)
    vec = lambda w: w[l].reshape(1, -1).astype(F32)
    return dict(
        ffn1=(mat(p["ffn1_w_gate"]), mat(p["ffn1_w_up"]), mat(p["ffn1_w_down"]),
              vec(p["ln1_g"]), vec(p["ln1_b"])),
        ffn2=(mat(p["ffn2_w_gate"]), mat(p["ffn2_w_up"]), mat(p["ffn2_w_down"]),
              vec(p["ln3_g"]), vec(p["ln3_b"])),
        inproj=(p["w_in"][l].astype(BF16), vec(p["b_in"]), vec(p["sgu_ln_g"]), vec(p["sgu_ln_b"]),
                p["sgu_w"][l].astype(BF16),
                jnp.broadcast_to(p["sgu_b"][l][:, :, None], (G_A, CHUNK, GA_DIM)).astype(F32)),
        logit_f=p["ret_logit_fwd"][l].astype(F32),
        logit_b=p["ret_logit_bwd"][l].astype(F32),
        gn=(vec(p["ret_gn_g"]), vec(p["ret_gn_b"])),
        combine=(p["w_br_a"][l].astype(BF16), p["w_br_b"][l].astype(BF16), p["w_br_c"][l].astype(BF16),
                 p["w_o"][l].astype(BF16), vec(p["ln2_g"]), vec(p["ln2_b"])),
    )


def _trunk(x, layers):
    batch, seq_len, d = x.shape
    x = x.reshape(batch * seq_len, d)
    cos2, sin2 = _rotary_tables(seq_len)
    tables = _fourier_tables(seq_len)
    for lw in layers:
        x = _ffn(x, *lw["ffn1"])
        gates, ya, q, k, v, g, zc, state_b = _inproj(x, seq_len, lw["logit_b"], *lw["inproj"], cos2, sin2)
        yb = _retention(lw["logit_f"], lw["logit_b"], q, k, v, g, state_b, *lw["gn"], seq_len)
        yc = _fourier(zc, tables)
        x = _combine(x, gates, ya, yb, yc, *lw["combine"])
        x = _ffn(x, *lw["ffn2"])
    return x.reshape(batch, seq_len, d)


def kernel(x_prompt, x_sample, ffn1_w_gate, ffn1_w_up, ffn1_w_down, ln1_g, ln1_b, w_in, b_in, sgu_ln_g, sgu_ln_b, sgu_w, sgu_b, ret_logit_fwd, ret_logit_bwd, ret_gn_g, ret_gn_b, w_br_a, w_br_b, w_br_c, w_o, ln2_g, ln2_b, ffn2_w_gate, ffn2_w_up, ffn2_w_down, ln3_g, ln3_b):
    p = dict(ffn1_w_gate=ffn1_w_gate, ffn1_w_up=ffn1_w_up, ffn1_w_down=ffn1_w_down, ln1_g=ln1_g,
             ln1_b=ln1_b, w_in=w_in, b_in=b_in, sgu_ln_g=sgu_ln_g, sgu_ln_b=sgu_ln_b, sgu_w=sgu_w,
             sgu_b=sgu_b, ret_logit_fwd=ret_logit_fwd, ret_logit_bwd=ret_logit_bwd, ret_gn_g=ret_gn_g,
             ret_gn_b=ret_gn_b, w_br_a=w_br_a, w_br_b=w_br_b, w_br_c=w_br_c, w_o=w_o, ln2_g=ln2_g,
             ln2_b=ln2_b, ffn2_w_gate=ffn2_w_gate, ffn2_w_up=ffn2_w_up, ffn2_w_down=ffn2_w_down,
             ln3_g=ln3_g, ln3_b=ln3_b)
    layers = [_prep_layer(p, l) for l in range(DEPTH)]
    return (_trunk(x_prompt, layers), _trunk(x_sample, layers))
```
